```python
import math
import jax, jax.numpy as jnp
from jax import lax
import numpy as np

D_MODEL = 2048
BATCH = 4
SEQ = 8192
DEPTH = 1

HEAD_DIM = 128
FOX_HEADS = D_MODEL // (2 * HEAD_DIM)
GDN_HEADS = D_MODEL // (2 * HEAD_DIM)
FOX_W = FOX_HEADS * HEAD_DIM
GDN_W = GDN_HEADS * HEAD_DIM
MIX_W = FOX_W + GDN_W
CONV_W = 4
GDN_CHUNK = 64
Q_BLOCK = 128
D_FF = 256 * ((8 * D_MODEL // 3 + 255) // 256)
N_MOD = 9
MACARON_W = 0.5
EPS = 1e-6

_SIZES = [FOX_W, FOX_W, FOX_W, FOX_HEADS,
          3 * GDN_W, GDN_HEADS, GDN_HEADS, GDN_W]
IN_W = sum(_SIZES)
SPLIT_IDX = tuple(int(v) for v in np.cumsum(_SIZES)[:-1])

kernel_name = "hybrid_fox_gdn_macaron_adaln"


def rmsnorm(x, g):
    xf = x.astype(jnp.float32)
    y = xf * lax.rsqrt(jnp.mean(xf * xf, axis=-1, keepdims=True) + EPS)
    return y * g.astype(jnp.float32)


def ada_in(x, g, shift, scale):
    y = rmsnorm(x, g) * (1.0 + scale[:, None, :].astype(jnp.float32)) + shift[:, None, :].astype(jnp.float32)
    return y.astype(x.dtype)


def swiglu(h, wg, wu, wd):
    return (jax.nn.silu(h @ wg) * (h @ wu)) @ wd


def l2norm(t):
    return t * lax.rsqrt(jnp.sum(t * t, axis=-1, keepdims=True) + EPS)


def causal_dwconv(x, w):
    k = w.shape[0]
    return lax.conv_general_dilated(
        x, w[:, None, :].astype(x.dtype), window_strides=(1,), padding=[(k - 1, 0)],
        dimension_numbers=("NWC", "WIO", "NWC"), feature_group_count=x.shape[-1])


def forgetting_attention(q, k, v, logf):
    b, h, s, d = q.shape
    nb = s // Q_BLOCK
    scale = 1.0 / math.sqrt(d)
    cum = jnp.cumsum(logf, axis=-1)
    qb = jnp.moveaxis(q.reshape(b, h, nb, Q_BLOCK, d), 2, 0)
    cb = jnp.moveaxis(cum.reshape(b, h, nb, Q_BLOCK), 2, 0)
    kpos = jnp.arange(s)

    def one_block(args):
        i, q_i, c_i = args
        logits = jnp.einsum("bhqd,bhkd->bhqk", q_i, k) * scale + c_i[..., None] - cum[:, :, None, :]
        qpos = i * Q_BLOCK + jnp.arange(Q_BLOCK)
        mask = kpos[None, :] <= qpos[:, None]
        p = jax.nn.softmax(jnp.where(mask, logits, -jnp.inf), axis=-1)
        return jnp.einsum("bhqk,bhkd->bhqd", p, v)

    o = lax.map(one_block, (jnp.arange(nb), qb, cb))
    return jnp.moveaxis(o, 0, 2).reshape(b, h, s, d)


def gated_delta_rule(q, k, v, g, beta):
    b, h, s, dk = q.shape
    dv = v.shape[-1]
    c = GDN_CHUNK
    n = s // c
    q = q * (dk ** -0.5)
    kb = k * beta[..., None]
    vb = v * beta[..., None]
    resh = lambda t: t.reshape(b, h, n, c, *t.shape[3:])
    q, k, kb, vb, g = resh(q), resh(k), resh(kb), resh(vb), resh(g)
    g = jnp.cumsum(g, axis=-1)
    incl = jnp.tril(jnp.ones((c, c), dtype=bool))
    strict = jnp.tril(jnp.ones((c, c), dtype=bool), -1)
    decay = jnp.exp(jnp.where(incl, g[..., :, None] - g[..., None, :], -jnp.inf))
    lower = jnp.where(strict, jnp.einsum("bhnid,bhnjd->bhnij", kb, k) * decay, 0.0)
    eye = jnp.eye(c, dtype=q.dtype)
    t_inv = lax.linalg.triangular_solve(eye + lower, jnp.broadcast_to(eye, lower.shape),
                                        left_side=True, lower=True, unit_diagonal=True)
    u = t_inv @ vb
    w = t_inv @ (kb * jnp.exp(g)[..., None])
    a_intra = jnp.where(incl, jnp.einsum("bhnid,bhnjd->bhnij", q, k) * decay, 0.0)

    def step(state, inp):
        q_i, k_i, u_i, w_i, g_i, a_i = inp
        v_new = u_i - w_i @ state
        o = (q_i * jnp.exp(g_i)[..., None]) @ state + a_i @ v_new
        g_last = g_i[..., -1]
        state = state * jnp.exp(g_last)[..., None, None] + jnp.einsum(
            "bhcd,bhce->bhde", k_i * jnp.exp(g_last[..., None] - g_i)[..., None], v_new)
        return state, o

    mv = lambda t: jnp.moveaxis(t, 2, 0)
    state0 = jnp.zeros((b, h, dk, dv), dtype=q.dtype)
    _, o = lax.scan(step, state0, (mv(q), mv(k), mv(u), mv(w), mv(g), mv(a_intra)))
    return jnp.moveaxis(o, 0, 2).reshape(b, h, s, dv)


def hybrid_mixer(h, w_in, w_out, fox_f_bias, fox_out_norm, gdn_conv, gdn_A_log, gdn_dt_bias, gdn_out_norm):
    bsz, s, _ = h.shape
    proj = (h @ w_in).astype(jnp.float32)
    q_f, k_f, v_f, f_f, qkv_g, a_g, b_g, z_g = jnp.split(proj, SPLIT_IDX, axis=-1)
    heads = lambda t, nh: t.reshape(bsz, s, nh, HEAD_DIM).transpose(0, 2, 1, 3)

    logf = jax.nn.log_sigmoid(f_f + fox_f_bias.astype(jnp.float32)).transpose(0, 2, 1)
    o_f = forgetting_attention(heads(q_f, FOX_HEADS), heads(k_f, FOX_HEADS), heads(v_f, FOX_HEADS), logf)
    o_f = rmsnorm(o_f.transpose(0, 2, 1, 3), fox_out_norm).reshape(bsz, s, FOX_W)

    qkv_g = jax.nn.silu(causal_dwconv(qkv_g, gdn_conv.astype(jnp.float32)))
    q_g, k_g, v_g = jnp.split(qkv_g, 3, axis=-1)
    q_g = l2norm(heads(q_g, GDN_HEADS))
    k_g = l2norm(heads(k_g, GDN_HEADS))
    v_g = heads(v_g, GDN_HEADS)
    g_log = (-jnp.exp(gdn_A_log.astype(jnp.float32))
             * jax.nn.softplus(a_g + gdn_dt_bias.astype(jnp.float32))).transpose(0, 2, 1)
    beta = jax.nn.sigmoid(b_g).transpose(0, 2, 1)
    o_g = gated_delta_rule(q_g, k_g, v_g, g_log, beta).transpose(0, 2, 1, 3)
    z = z_g.reshape(bsz, s, GDN_HEADS, HEAD_DIM)
    o_g = (rmsnorm(o_g, gdn_out_norm) * jax.nn.silu(z)).reshape(bsz, s, GDN_W)

    o = jnp.concatenate([o_f, o_g], axis=-1).astype(h.dtype)
    return o @ w_out


def setup_inputs(seed: int = 0) -> dict:
    key = jax.random.key(seed)
    ks = jax.random.split(key, 20)
    nrm = lambda k, shape, s: jax.random.normal(k, shape, jnp.float32) * s
    x = nrm(ks[0], (BATCH, SEQ, D_MODEL), 1.0)
    c = nrm(ks[1], (BATCH, D_MODEL), 1.0)
    ada_w = nrm(ks[2], (DEPTH, D_MODEL, N_MOD * D_MODEL), 0.5 * D_MODEL ** -0.5)
    ada_b = nrm(ks[3], (DEPTH, N_MOD * D_MODEL), 0.1)
    norm_g = 1.0 + nrm(ks[4], (DEPTH, 3, D_MODEL), 0.1)
    ffn_w_gate = nrm(ks[5], (DEPTH, 2, D_MODEL, D_FF), D_MODEL ** -0.5)
    ffn_w_up = nrm(ks[6], (DEPTH, 2, D_MODEL, D_FF), D_MODEL ** -0.5)
    ffn_w_down = nrm(ks[7], (DEPTH, 2, D_FF, D_MODEL), D_FF ** -0.5)
    w_in = nrm(ks[8], (DEPTH, D_MODEL, IN_W), D_MODEL ** -0.5)
    w_out = nrm(ks[9], (DEPTH, MIX_W, D_MODEL), MIX_W ** -0.5)
    fox_f_bias = 3.0 + nrm(ks[10], (DEPTH, FOX_HEADS), 0.5)
    fox_out_norm = 1.0 + nrm(ks[11], (DEPTH, HEAD_DIM), 0.1)
    gdn_conv = nrm(ks[12], (DEPTH, CONV_W, 3 * GDN_W), CONV_W ** -0.5)
    gdn_A_log = jnp.log(jax.random.uniform(ks[13], (DEPTH, GDN_HEADS), jnp.float32, 1.0, 16.0))
    dt = jnp.exp(jax.random.uniform(ks[14], (DEPTH, GDN_HEADS), jnp.float32,
                                    math.log(1e-3), math.log(1e-1)))
    gdn_dt_bias = dt + jnp.log(-jnp.expm1(-dt))
    gdn_out_norm = 1.0 + nrm(ks[15], (DEPTH, HEAD_DIM), 0.1)
    final_norm = 1.0 + nrm(ks[16], (D_MODEL,), 0.1)
    return {"x": x, "c": c, "ada_w": ada_w, "ada_b": ada_b, "norm_g": norm_g,
            "ffn_w_gate": ffn_w_gate, "ffn_w_up": ffn_w_up, "ffn_w_down": ffn_w_down,
            "w_in": w_in, "w_out": w_out, "fox_f_bias": fox_f_bias, "fox_out_norm": fox_out_norm,
            "gdn_conv": gdn_conv, "gdn_A_log": gdn_A_log, "gdn_dt_bias": gdn_dt_bias,
            "gdn_out_norm": gdn_out_norm, "final_norm": final_norm}


def reference(x, c, ada_w, ada_b, norm_g, ffn_w_gate, ffn_w_up, ffn_w_down, w_in, w_out,
              fox_f_bias, fox_out_norm, gdn_conv, gdn_A_log, gdn_dt_bias, gdn_out_norm, final_norm):
    cond = jax.nn.silu(c)
    for l in range(DEPTH):
        mod = cond @ ada_w[l] + ada_b[l]
        sh1, sc1, gt1, sh2, sc2, gt2, sh3, sc3, gt3 = jnp.split(mod, N_MOD, axis=-1)
        h = ada_in(x, norm_g[l, 0], sh1, sc1)
        x = x + MACARON_W * gt1[:, None, :] * swiglu(h, ffn_w_gate[l, 0], ffn_w_up[l, 0], ffn_w_down[l, 0])
        h = ada_in(x, norm_g[l, 1], sh2, sc2)
        x = x + gt2[:, None, :] * hybrid_mixer(h, w_in[l], w_out[l], fox_f_bias[l], fox_out_norm[l],
                                               gdn_conv[l], gdn_A_log[l], gdn_dt_bias[l], gdn_out_norm[l])
        h = ada_in(x, norm_g[l, 2], sh3, sc3)
        x = x + MACARON_W * gt3[:, None, :] * swiglu(h, ffn_w_gate[l, 1], ffn_w_up[l, 1], ffn_w_down[l, 1])
    return rmsnorm(x, final_norm).astype(x.dtype)
```

```python
import functools
import math

import jax
import jax.numpy as jnp
from jax import lax
from jax.experimental import pallas as pl
from jax.experimental.pallas import tpu as pltpu

F32 = jnp.float32
BF16 = jnp.bfloat16

D_MODEL = 2048
HEAD_DIM = 128
N_HEADS = 8
GROUP_W = N_HEADS * HEAD_DIM
D_FF = 5632
CONV_W = 4
N_MOD = 9
EPS = 1e-6
LANES = 128
SUBLANES = 8
GDN_CHUNK = 128

_OFF_QF = 0
_OFF_FF = 3 * GROUP_W
_OFF_QKVG = _OFF_FF + N_HEADS
_OFF_AG = _OFF_QKVG + 3 * GROUP_W
_OFF_BG = _OFF_AG + N_HEADS
_OFF_ZG = _OFF_BG + N_HEADS

VMEM_LIMIT = 56 * 1024 * 1024


def _params(*sem):
    return pltpu.CompilerParams(dimension_semantics=sem, vmem_limit_bytes=VMEM_LIMIT)


def _sigmoid(x):
    return 1.0 / (1.0 + jnp.exp(-x))


def _silu(x):
    return x * _sigmoid(x)


def _softplus(x):
    return jnp.maximum(x, 0.0) + jnp.log1p(jnp.exp(-jnp.abs(x)))


def _dot(a, b):
    return jnp.dot(a, b, preferred_element_type=F32)


def _dot_nt(a, b):
    return lax.dot_general(a, b, (((1,), (1,)), ((), ())), preferred_element_type=F32)


def _dot_tn(a, b):
    return lax.dot_general(a, b, (((0,), (0,)), ((), ())), preferred_element_type=F32)


def _rms(x):
    return x * lax.rsqrt(jnp.mean(x * x, axis=-1, keepdims=True) + EPS)


def _ada_kernel(c_ref, w_ref, b_ref, o_ref):
    cond = _silu(c_ref[...])
    o_ref[...] = _dot(cond.astype(BF16), w_ref[...].astype(BF16)) + b_ref[...]


def _ada(c_pad, w, b):
    rows, d = c_pad.shape
    n = w.shape[1]
    tn = 1024
    return pl.pallas_call(
        _ada_kernel,
        grid=(n // tn,),
        in_specs=[pl.BlockSpec((rows, d), lambda j: (0, 0)),
                  pl.BlockSpec((d, tn), lambda j: (0, j)),
                  pl.BlockSpec((1, tn), lambda j: (0, j))],
        out_specs=pl.BlockSpec((rows, tn), lambda j: (0, j)),
        out_shape=jax.ShapeDtypeStruct((rows, n), F32),
        compiler_params=_params("arbitrary"),
        name="ada",
    )(c_pad, w, b)


def _ffn_kernel(final, x_ref, g_ref, sh_ref, sc_ref, gt_ref, wg_ref, wu_ref, wd_ref,
                g2_ref, sh2_ref, sc2_ref, *rest):
    if final:
        o_ref, h_scr, acc_scr = rest
    else:
        o_ref, h2_ref, h_scr, acc_scr = rest
    f = pl.program_id(1)

    @pl.when(f == 0)
    def _():
        y = _rms(x_ref[...]) * g_ref[...]
        h_scr[...] = (y * (1.0 + sc_ref[0]) + sh_ref[0]).astype(BF16)
        acc_scr[...] = jnp.zeros_like(acc_scr)

    h = h_scr[...]
    act = _silu(_dot(h, wg_ref[...])) * _dot(h, wu_ref[...])
    acc_scr[...] += _dot(act.astype(BF16), wd_ref[...])

    @pl.when(f == pl.num_programs(1) - 1)
    def _():
        xn = x_ref[...] + 0.5 * gt_ref[0] * acc_scr[...]
        if final:
            o_ref[...] = _rms(xn) * g2_ref[...]
        else:
            o_ref[...] = xn
            y2 = _rms(xn) * g2_ref[...]
            h2_ref[...] = (y2 * (1.0 + sc2_ref[0]) + sh2_ref[0]).astype(BF16)


def _ffn(x, mod, tiles_per_batch, tm, g, k_sh, wg, wu, wd, g2, k_sh2, final):
    n, d = x.shape
    dff = wg.shape[1]
    tf = 512
    row = lambda i, f: (i, 0)
    modspec = lambda k: pl.BlockSpec((1, 1, d), lambda i, f: ((i // tiles_per_batch) * N_MOD + k, 0, 0))
    vec = pl.BlockSpec((1, d), lambda i, f: (0, 0))
    in_specs = [pl.BlockSpec((tm, d), row), vec, modspec(k_sh), modspec(k_sh + 1), modspec(k_sh + 2),
                pl.BlockSpec((d, tf), lambda i, f: (0, f)),
                pl.BlockSpec((d, tf), lambda i, f: (0, f)),
                pl.BlockSpec((tf, d), lambda i, f: (f, 0)),
                vec, modspec(k_sh2), modspec(k_sh2 + 1)]
    if final:
        out_shape = jax.ShapeDtypeStruct((n, d), F32)
        out_specs = pl.BlockSpec((tm, d), row)
    else:
        out_shape = (jax.ShapeDtypeStruct((n, d), F32), jax.ShapeDtypeStruct((n, d), BF16))
        out_specs = (pl.BlockSpec((tm, d), row), pl.BlockSpec((tm, d), row))
    return pl.pallas_call(
        functools.partial(_ffn_kernel, final),
        grid=(n // tm, dff // tf),
        in_specs=in_specs,
        out_specs=out_specs,
        out_shape=out_shape,
        scratch_shapes=[pltpu.VMEM((tm, d), BF16), pltpu.VMEM((tm, d), F32)],
        compiler_params=_params("arbitrary", "arbitrary"),
        name="ffn_final" if final else "ffn",
    )(x, g, mod, mod, mod, wg, wu, wd, g2, mod, mod)


def _proj_kernel(h_ref, w_ref, o_ref):
    o_ref[...] = _dot(h_ref[...], w_ref[...]).astype(o_ref.dtype)


def _proj_f(h, w, tm):
    n, d = h.shape
    c = w.shape[1]
    tn = 1024
    return pl.pallas_call(
        _proj_kernel,
        grid=(c // tn, n // tm),
        in_specs=[pl.BlockSpec((tm, d), lambda j, i: (i, 0)),
                  pl.BlockSpec((d, tn), lambda j, i: (0, j))],
        out_specs=pl.BlockSpec((tm, tn), lambda j, i: (i, j)),
        out_shape=jax.ShapeDtypeStruct((n, c), BF16),
        compiler_params=_params("arbitrary", "arbitrary"),
        name="proj_f",
    )(h, w)


def _proj_g_kernel(n_conv, n_norm, tn, h_ref, w_ref, cw_ref, o_ref, prev_scr):
    j = pl.program_id(0)
    s = pl.program_id(2)
    y = _dot(h_ref[...], w_ref[...])
    tm = y.shape[0]

    @pl.when(j < n_conv)
    def _():
        @pl.when(s == 0)
        def _():
            prev_scr[...] = jnp.zeros_like(prev_scr)

        prev = prev_scr[...]
        row8 = lax.broadcasted_iota(jnp.int32, (SUBLANES, tn), 0)
        cw = cw_ref[...]
        out = y * cw[CONV_W - 1:CONV_W, :]
        for sh in range(1, CONV_W):
            rolled = pltpu.roll(y, sh, 0)
            head = jnp.where(row8 < sh, pltpu.roll(prev, sh, 0), rolled[:SUBLANES])
            shifted = jnp.concatenate([head, rolled[SUBLANES:]], axis=0)
            out = out + shifted * cw[CONV_W - 1 - sh:CONV_W - sh, :]
        prev_scr[...] = y[tm - SUBLANES:, :]
        out = _silu(out)
        is_q = j * tn < GROUP_W
        scale = jnp.where(is_q, HEAD_DIM ** -0.5, 1.0)
        do_norm = j < n_norm
        for hh in range(tn // HEAD_DIM):
            t = out[:, hh * HEAD_DIM:(hh + 1) * HEAD_DIM]
            nrm = lax.rsqrt(jnp.sum(t * t, axis=-1, keepdims=True) + EPS) * scale
            o_ref[0, :, hh * HEAD_DIM:(hh + 1) * HEAD_DIM] = t * jnp.where(do_norm, nrm, 1.0)

    @pl.when(j >= n_conv)
    def _():
        o_ref[0] = _silu(y)


def _proj_g(h3, w, cw, tm):
    b, s, d = h3.shape
    c = w.shape[1]
    tn = 512
    n_conv = 3 * GROUP_W // tn
    n_norm = 2 * GROUP_W // tn
    return pl.pallas_call(
        functools.partial(_proj_g_kernel, n_conv, n_norm, tn),
        grid=(c // tn, b, s // tm),
        in_specs=[pl.BlockSpec((None, tm, d), lambda j, bb, i: (bb, i, 0)),
                  pl.BlockSpec((d, tn), lambda j, bb, i: (0, j)),
                  pl.BlockSpec((SUBLANES, tn), lambda j, bb, i: (0, j))],
        out_specs=pl.BlockSpec((1, tm, tn), lambda j, bb, i: (bb, i, j)),
        out_shape=jax.ShapeDtypeStruct((b, s, c), F32),
        scratch_shapes=[pltpu.VMEM((SUBLANES, tn), F32)],
        compiler_params=_params("arbitrary", "arbitrary", "arbitrary"),
        name="proj_g",
    )(h3, w, cw)


def _gates_kernel(h_ref, w_ref, bias_ref, alog_ref, col_ref, row_ref, carry_scr):
    s = pl.program_id(1)
    y = _dot(h_ref[...], w_ref[...])
    tm = y.shape[0]
    lane = lax.broadcasted_iota(jnp.int32, (1, LANES), 1)
    is_f = lane < N_HEADS
    is_a = (lane >= N_HEADS) & (lane < 2 * N_HEADS)
    yb = y + bias_ref[...]
    sp = _softplus(jnp.where(is_f, -yb, yb))
    val = jnp.where(is_f, -sp, jnp.where(is_a, -jnp.exp(alog_ref[...]) * sp, _sigmoid(y)))

    @pl.when(s == 0)
    def _():
        carry_scr[...] = jnp.zeros_like(carry_scr)

    rowi = lax.broadcasted_iota(jnp.int32, (tm, LANES), 0)
    rowc = rowi % GDN_CHUNK
    cum = val
    cumc = val
    sh = 1
    while sh < tm:
        cum = cum + jnp.where(rowi >= sh, pltpu.roll(cum, sh, 0), 0.0)
        if sh < GDN_CHUNK:
            cumc = cumc + jnp.where(rowc >= sh, pltpu.roll(cumc, sh, 0), 0.0)
        sh *= 2
    cum = cum + carry_scr[0:1, :]
    carry_scr[...] = jnp.broadcast_to(cum[tm - 1:tm, :], carry_scr.shape)
    out = jnp.where(is_f, cum, jnp.where(is_a, cumc, val))
    col_ref[0] = out
    row_ref[0] = out.T[:row_ref.shape[1], :]


def _gates(h3, w, bias, alog, tm):
    b, s, d = h3.shape
    nrow = 32
    return pl.pallas_call(
        _gates_kernel,
        grid=(b, s // tm),
        in_specs=[pl.BlockSpec((None, tm, d), lambda bb, i: (bb, i, 0)),
                  pl.BlockSpec((d, LANES), lambda bb, i: (0, 0)),
                  pl.BlockSpec((1, LANES), lambda bb, i: (0, 0)),
                  pl.BlockSpec((1, LANES), lambda bb, i: (0, 0))],
        out_specs=(pl.BlockSpec((1, tm, LANES), lambda bb, i: (bb, i, 0)),
                   pl.BlockSpec((1, nrow, tm), lambda bb, i: (bb, 0, i))),
        out_shape=(jax.ShapeDtypeStruct((b, s, LANES), F32),
                   jax.ShapeDtypeStruct((b, nrow, s), F32)),
        scratch_shapes=[pltpu.VMEM((SUBLANES, LANES), F32)],
        compiler_params=_params("arbitrary", "arbitrary"),
        name="gates",
    )(h3, w, bias, alog)


def _pick_lane(x, idx):
    lane = lax.broadcasted_iota(jnp.int32, x.shape, 1)
    return jnp.sum(jnp.where(lane == idx, x, 0.0), axis=-1, keepdims=True)


def _attn_kernel(tq, q_ref, k_ref, v_ref, cq_ref, ck_ref, gn_ref, o_ref):
    h = pl.program_id(1)
    qi = pl.program_id(2)
    scale = 1.0 / math.sqrt(HEAD_DIM)
    q = q_ref[0]
    cq = _pick_lane(cq_ref[0], h)

    def block(j, carry, masked):
        m, l, acc = carry
        off = pl.multiple_of(j * tq, tq)
        k = k_ref[0, pl.ds(off, tq), :]
        v = v_ref[0, pl.ds(off, tq), :]
        ck = ck_ref[0, 0, :, pl.ds(off, tq)]
        s = _dot_nt(q, k) * scale + (cq - ck)
        if masked:
            r = lax.broadcasted_iota(jnp.int32, (tq, tq), 0)
            c = lax.broadcasted_iota(jnp.int32, (tq, tq), 1)
            s = jnp.where(c <= r, s, -jnp.inf)
        m_new = jnp.maximum(m, jnp.max(s, axis=-1, keepdims=True))
        p = jnp.exp(s - m_new)
        alpha = jnp.exp(m - m_new)
        l = alpha * l + jnp.sum(p, axis=-1, keepdims=True)
        acc = alpha * acc + _dot(p.astype(BF16), v)
        return m_new, l, acc

    init = (jnp.full((tq, 1), -jnp.inf, F32), jnp.zeros((tq, 1), F32), jnp.zeros((tq, HEAD_DIM), F32))
    carry = lax.fori_loop(0, qi, lambda j, c: block(j, c, False), init)
    _, l, acc = block(qi, carry, True)
    o_ref[0] = (_rms(acc / l) * gn_ref[...]).astype(o_ref.dtype)


def _attn(qkv, col, row, gn, tq):
    b, s, _ = qkv.shape
    return pl.pallas_call(
        functools.partial(_attn_kernel, tq),
        grid=(b, N_HEADS, s // tq),
        in_specs=[pl.BlockSpec((1, tq, HEAD_DIM), lambda bb, h, i: (bb, i, h)),
                  pl.BlockSpec((1, s, HEAD_DIM), lambda bb, h, i: (bb, 0, N_HEADS + h)),
                  pl.BlockSpec((1, s, HEAD_DIM), lambda bb, h, i: (bb, 0, 2 * N_HEADS + h)),
                  pl.BlockSpec((1, tq, LANES), lambda bb, h, i: (bb, i, 0)),
                  pl.BlockSpec((1, 1, 1, s), lambda bb, h, i: (bb, h, 0, 0)),
                  pl.BlockSpec((1, HEAD_DIM), lambda bb, h, i: (0, 0))],
        out_specs=pl.BlockSpec((1, tq, HEAD_DIM), lambda bb, h, i: (bb, i, h)),
        out_shape=jax.ShapeDtypeStruct((b, s, GROUP_W), BF16),
        compiler_params=_params("arbitrary", "arbitrary", "arbitrary"),
        name="attn",
    )(qkv, qkv, qkv, col, row, gn)


def _unit_lower_inverse(lower):
    c = lower.shape[0]
    r = lax.broadcasted_iota(jnp.int32, (c, c), 0)
    cc = lax.broadcasted_iota(jnp.int32, (c, c), 1)
    inv = jnp.where(r == cc, 1.0, 0.0) - jnp.where((r ^ cc) == 1, lower, 0.0)
    level = 1
    while (1 << level) < c:
        off = jnp.where(((r >> level) ^ (cc >> level)) == 1, lower, 0.0).astype(BF16)
        inv_b = inv.astype(BF16)
        inv = inv - _dot(_dot(inv_b, off).astype(BF16), inv_b)
        level += 1
    return inv


def _gdn_kernel(t_tokens, qkvz_refs, col_ref, row_ref, gn_ref, o_ref, state_scr):
    q_ref, k_ref, v_ref, z_ref = qkvz_refs
    h = pl.program_id(1)
    t = pl.program_id(2)
    c = GDN_CHUNK

    @pl.when(t == 0)
    def _():
        state_scr[...] = jnp.zeros_like(state_scr)

    col = col_ref[0]
    g_all = _pick_lane(col, N_HEADS + h)
    b_all = _pick_lane(col, 2 * N_HEADS + h)
    r = lax.broadcasted_iota(jnp.int32, (c, c), 0)
    cc = lax.broadcasted_iota(jnp.int32, (c, c), 1)
    incl = cc <= r
    strict = cc < r
    state = state_scr[...]
    for ci in range(t_tokens // c):
        sl = slice(ci * c, (ci + 1) * c)
        q = q_ref[0, sl, :]
        k = k_ref[0, sl, :]
        v = v_ref[0, sl, :]
        g = g_all[sl]
        beta = b_all[sl]
        g_row = row_ref[0, 0, :, sl]
        decay = jnp.exp(jnp.where(incl, g - g_row, -jnp.inf))
        kb = k * beta
        k_b = k.astype(BF16)
        eg = jnp.exp(g)
        lower = jnp.where(strict, _dot_nt(kb.astype(BF16), k_b) * decay, 0.0)
        t_inv = _unit_lower_inverse(lower).astype(BF16)
        u = _dot(t_inv, (v * beta).astype(BF16))
        w = _dot(t_inv, (kb * eg).astype(BF16))
        a = jnp.where(incl, _dot_nt(q.astype(BF16), k_b) * decay, 0.0)
        state_b = state.astype(BF16)
        v_new = u - _dot(w.astype(BF16), state_b)
        v_new_b = v_new.astype(BF16)
        o = _dot((q * eg).astype(BF16), state_b) + _dot(a.astype(BF16), v_new_b)
        g_last = g[c - 1:c, :]
        state = state * jnp.exp(g_last) + _dot_tn((k * jnp.exp(g_last - g)).astype(BF16), v_new_b)
        o_ref[0, sl, :] = (_rms(o) * gn_ref[...] * z_ref[0, sl, :]).astype(o_ref.dtype)
    state_scr[...] = state


def _gdn(qkvz, col, row, gn, t_tokens):
    b, s, _ = qkvz.shape
    head = lambda g: pl.BlockSpec((1, t_tokens, HEAD_DIM), lambda bb, h, i: (bb, i, g * N_HEADS + h))

    def kern(q_ref, k_ref, v_ref, z_ref, col_ref, row_ref, gn_ref, o_ref, state_scr):
        _gdn_kernel(t_tokens, (q_ref, k_ref, v_ref, z_ref), col_ref, row_ref, gn_ref, o_ref, state_scr)

    return pl.pallas_call(
        kern,
        grid=(b, N_HEADS, s // t_tokens),
        in_specs=[head(0), head(1), head(2), head(3),
                  pl.BlockSpec((1, t_tokens, LANES), lambda bb, h, i: (bb, i, 0)),
                  pl.BlockSpec((1, 1, 1, t_tokens), lambda bb, h, i: (bb, N_HEADS + h, 0, i)),
                  pl.BlockSpec((1, HEAD_DIM), lambda bb, h, i: (0, 0))],
        out_specs=pl.BlockSpec((1, t_tokens, HEAD_DIM), lambda bb, h, i: (bb, i, h)),
        out_shape=jax.ShapeDtypeStruct((b, s, GROUP_W), BF16),
        scratch_shapes=[pltpu.VMEM((HEAD_DIM, HEAD_DIM), F32)],
        compiler_params=_params("arbitrary", "arbitrary", "arbitrary"),
        name="gdn",
    )(qkvz, qkvz, qkvz, qkvz, col, row, gn)


def _oproj_kernel(of_ref, og_ref, wf_ref, wg_ref, x_ref, gt_ref, o_ref):
    y = _dot(of_ref[...], wf_ref[...]) + _dot(og_ref[...], wg_ref[...])
    o_ref[...] = x_ref[...] + gt_ref[0] * y


def _oproj(o_f, o_g, w_f, w_g, x, mod, tiles_per_batch, tm):
    n, d = x.shape
    row = lambda i: (i, 0)
    return pl.pallas_call(
        _oproj_kernel,
        grid=(n // tm,),
        in_specs=[pl.BlockSpec((tm, GROUP_W), row), pl.BlockSpec((tm, GROUP_W), row),
                  pl.BlockSpec((GROUP_W, d), lambda i: (0, 0)), pl.BlockSpec((GROUP_W, d), lambda i: (0, 0)),
                  pl.BlockSpec((tm, d), row),
                  pl.BlockSpec((1, 1, d), lambda i: ((i // tiles_per_batch) * N_MOD + 5, 0, 0))],
        out_specs=pl.BlockSpec((tm, d), row),
        out_shape=jax.ShapeDtypeStruct((n, d), F32),
        compiler_params=_params("arbitrary"),
        name="oproj",
    )(o_f, o_g, w_f, w_g, x, mod)


def kernel(x, c, ada_w, ada_b, norm_g, ffn_w_gate, ffn_w_up, ffn_w_down, w_in, w_out,
           fox_f_bias, fox_out_norm, gdn_conv, gdn_A_log, gdn_dt_bias, gdn_out_norm, final_norm):
    bsz, s, d = x.shape
    n = bsz * s
    tm = 512
    tiles_per_batch = s // tm
    l = 0
    bf = lambda t: t.astype(BF16)

    c_pad = jnp.zeros((SUBLANES, d), F32).at[:bsz].set(c)
    mod = _ada(c_pad, ada_w[l], ada_b[l][None, :])[:bsz].reshape(bsz * N_MOD, 1, d)

    xf = x.reshape(n, d)
    x1, h2 = _ffn(xf, mod, tiles_per_batch, tm, norm_g[l, 0][None], 0,
                  bf(ffn_w_gate[l, 0]), bf(ffn_w_up[l, 0]), bf(ffn_w_down[l, 0]),
                  norm_g[l, 1][None], 3, final=False)

    w = w_in[l]
    qkv_f = _proj_f(h2, bf(w[:, _OFF_QF:_OFF_FF]), 1024).reshape(bsz, s, 3 * GROUP_W)
    h3 = h2.reshape(bsz, s, d)
    w_g = bf(jnp.concatenate([w[:, _OFF_QKVG:_OFF_AG], w[:, _OFF_ZG:]], axis=1))
    cw = jnp.zeros((SUBLANES, 4 * GROUP_W), F32).at[:CONV_W, :3 * GROUP_W].set(gdn_conv[l])
    qkvz_g = _proj_g(h3, w_g, cw, tm)

    pad = LANES - 3 * N_HEADS
    w_s = bf(jnp.concatenate([w[:, _OFF_FF:_OFF_QKVG], w[:, _OFF_AG:_OFF_ZG], jnp.zeros((d, pad), F32)], axis=1))
    zeros8 = jnp.zeros((N_HEADS,), F32)
    bias = jnp.concatenate([fox_f_bias[l], gdn_dt_bias[l], zeros8, jnp.zeros((pad,), F32)])[None]
    alog = jnp.concatenate([zeros8, gdn_A_log[l], zeros8, jnp.zeros((pad,), F32)])[None]
    col, row = _gates(h3, w_s, bias, alog, tm)
    row = row.reshape(bsz, row.shape[1], 1, s)

    o_f = _attn(qkv_f, col, row, fox_out_norm[l][None], 512)
    o_g = _gdn(qkvz_g, col, row, gdn_out_norm[l][None], 512)

    wo = bf(w_out[l])
    x2 = _oproj(o_f.reshape(n, GROUP_W), o_g.reshape(n, GROUP_W), wo[:GROUP_W], wo[GROUP_W:],
                x1, mod, tiles_per_batch, tm)

    out = _ffn(x2, mod, tiles_per_batch, tm, norm_g[l, 2][None], 6,
               bf(ffn_w_gate[l, 1]), bf(ffn_w_up[l, 1]), bf(ffn_w_down[l, 1]),
               final_norm[None], 0, final=True)
    return out.reshape(bsz, s, d)
```

```python
import functools
import math

import jax
import jax.numpy as jnp
from jax import lax
from jax.experimental import pallas as pl
from jax.experimental.pallas import tpu as pltpu

F32 = jnp.float32
BF16 = jnp.bfloat16

D_MODEL = 2048
HEAD_DIM = 128
N_HEADS = 8
GROUP_W = N_HEADS * HEAD_DIM
D_FF = 5632
CONV_W = 4
N_MOD = 9
EPS = 1e-6
LANES = 128
SUBLANES = 8
GDN_CHUNK = 128
LOG2E = math.log2(math.e)

_OFF_QF = 0
_OFF_FF = 3 * GROUP_W
_OFF_QKVG = _OFF_FF + N_HEADS
_OFF_AG = _OFF_QKVG + 3 * GROUP_W
_OFF_BG = _OFF_AG + N_HEADS
_OFF_ZG = _OFF_BG + N_HEADS

VMEM_LIMIT = 56 * 1024 * 1024


def _params(*sem):
    return pltpu.CompilerParams(dimension_semantics=sem, vmem_limit_bytes=VMEM_LIMIT)


def _sigmoid(x):
    return 1.0 / (1.0 + jnp.exp(-x))


def _silu(x):
    return x * _sigmoid(x)


def _softplus(x):
    return jnp.maximum(x, 0.0) + jnp.log1p(jnp.exp(-jnp.abs(x)))


def _dot(a, b):
    return jnp.dot(a, b, preferred_element_type=F32)


def _dot_nt(a, b):
    return lax.dot_general(a, b, (((1,), (1,)), ((), ())), preferred_element_type=F32)


def _dot_tn(a, b):
    return lax.dot_general(a, b, (((0,), (0,)), ((), ())), preferred_element_type=F32)


def _rms(x):
    return x * lax.rsqrt(jnp.mean(x * x, axis=-1, keepdims=True) + EPS)


def _ada_kernel(c_ref, w_ref, b_ref, o_ref):
    cond = _silu(c_ref[...])
    o_ref[...] = _dot(cond.astype(BF16), w_ref[...].astype(BF16)) + b_ref[...]


def _ada(c_pad, w, b):
    rows, d = c_pad.shape
    n = w.shape[1]
    tn = 1024
    return pl.pallas_call(
        _ada_kernel,
        grid=(n // tn,),
        in_specs=[pl.BlockSpec((rows, d), lambda j: (0, 0)),
                  pl.BlockSpec((d, tn), lambda j: (0, j)),
                  pl.BlockSpec((1, tn), lambda j: (0, j))],
        out_specs=pl.BlockSpec((rows, tn), lambda j: (0, j)),
        out_shape=jax.ShapeDtypeStruct((rows, n), F32),
        compiler_params=_params("arbitrary"),
        name="ada",
    )(c_pad, w, b)


def _ffn_kernel(final, x_ref, g_ref, sh_ref, sc_ref, gt_ref, wg_ref, wu_ref, wd_ref,
                g2_ref, sh2_ref, sc2_ref, *rest):
    if final:
        o_ref, h_scr, acc_scr = rest
    else:
        o_ref, h2_ref, h_scr, acc_scr = rest
    f = pl.program_id(1)

    @pl.when(f == 0)
    def _():
        y = _rms(x_ref[...]) * g_ref[...]
        h_scr[...] = (y * (1.0 + sc_ref[0]) + sh_ref[0]).astype(BF16)
        acc_scr[...] = jnp.zeros_like(acc_scr)

    h = h_scr[...]
    act = _silu(_dot(h, wg_ref[...])) * _dot(h, wu_ref[...])
    acc_scr[...] += _dot(act.astype(BF16), wd_ref[...])

    @pl.when(f == pl.num_programs(1) - 1)
    def _():
        xn = x_ref[...] + 0.5 * gt_ref[0] * acc_scr[...]
        if final:
            o_ref[...] = _rms(xn) * g2_ref[...]
        else:
            o_ref[...] = xn
            y2 = _rms(xn) * g2_ref[...]
            h2_ref[...] = (y2 * (1.0 + sc2_ref[0]) + sh2_ref[0]).astype(BF16)


def _ffn(x, mod, tiles_per_batch, tm, g, k_sh, wg, wu, wd, g2, k_sh2, final):
    n, d = x.shape
    dff = wg.shape[1]
    tf = 512
    row = lambda i, f: (i, 0)
    modspec = lambda k: pl.BlockSpec((1, 1, d), lambda i, f: ((i // tiles_per_batch) * N_MOD + k, 0, 0))
    vec = pl.BlockSpec((1, d), lambda i, f: (0, 0))
    in_specs = [pl.BlockSpec((tm, d), row), vec, modspec(k_sh), modspec(k_sh + 1), modspec(k_sh + 2),
                pl.BlockSpec((d, tf), lambda i, f: (0, f)),
                pl.BlockSpec((d, tf), lambda i, f: (0, f)),
                pl.BlockSpec((tf, d), lambda i, f: (f, 0)),
                vec, modspec(k_sh2), modspec(k_sh2 + 1)]
    if final:
        out_shape = jax.ShapeDtypeStruct((n, d), F32)
        out_specs = pl.BlockSpec((tm, d), row)
    else:
        out_shape = (jax.ShapeDtypeStruct((n, d), F32), jax.ShapeDtypeStruct((n, d), BF16))
        out_specs = (pl.BlockSpec((tm, d), row), pl.BlockSpec((tm, d), row))
    return pl.pallas_call(
        functools.partial(_ffn_kernel, final),
        grid=(n // tm, dff // tf),
        in_specs=in_specs,
        out_specs=out_specs,
        out_shape=out_shape,
        scratch_shapes=[pltpu.VMEM((tm, d), BF16), pltpu.VMEM((tm, d), F32)],
        compiler_params=_params("arbitrary", "arbitrary"),
        name="ffn_final" if final else "ffn",
    )(x, g, mod, mod, mod, wg, wu, wd, g2, mod, mod)


def _proj_kernel(gated, h_ref, w_ref, o_ref):
    y = _dot(h_ref[...], w_ref[...])
    if gated:
        y = _silu(y)
    else:
        y = y * jnp.where(pl.program_id(0) == 0, LOG2E / math.sqrt(HEAD_DIM), 1.0)
    o_ref[...] = y.astype(o_ref.dtype)


def _proj(h, w, tm, gated, out_dtype, name):
    n, d = h.shape
    c = w.shape[1]
    tn = GROUP_W
    return pl.pallas_call(
        functools.partial(_proj_kernel, gated),
        grid=(c // tn, n // tm),
        in_specs=[pl.BlockSpec((tm, d), lambda j, i: (i, 0)),
                  pl.BlockSpec((d, tn), lambda j, i: (0, j))],
        out_specs=pl.BlockSpec((tm, tn), lambda j, i: (i, j)),
        out_shape=jax.ShapeDtypeStruct((n, c), out_dtype),
        compiler_params=_params("arbitrary", "arbitrary"),
        name=name,
    )(h, w)


_CONV_COLS = 2 * HEAD_DIM


def _proj_g_kernel(h_ref, w_ref, cw_ref, o_ref, prev_scr, *bufs):
    j = pl.program_id(0)
    s = pl.program_id(2)
    tm = h_ref.shape[0]
    h = h_ref[...]
    cw = cw_ref[...]
    prev = jnp.where(s == 0, 0.0, prev_scr[...])
    gain = jnp.where(j == 0, HEAD_DIM ** -0.5, 1.0)
    for p, buf in enumerate(bufs):
        cols = slice(p * _CONV_COLS, (p + 1) * _CONV_COLS)
        y = _dot(h, w_ref[:, cols])
        buf[0:SUBLANES, :] = prev[:, cols]
        buf[SUBLANES:, :] = y
        prev_scr[:, cols] = y[tm - SUBLANES:, :]
        out = y * cw[CONV_W - 1:CONV_W, cols]
        for sh in range(1, CONV_W):
            out = out + buf[SUBLANES - sh:SUBLANES - sh + tm, :] * cw[CONV_W - 1 - sh:CONV_W - sh, cols]
        out = _silu(out)
        for hh in range(_CONV_COLS // HEAD_DIM):
            t = out[:, hh * HEAD_DIM:(hh + 1) * HEAD_DIM]
            nrm = lax.rsqrt(jnp.sum(t * t, axis=-1, keepdims=True) + EPS) * gain
            c0 = p * _CONV_COLS + hh * HEAD_DIM
            o_ref[0, :, c0:c0 + HEAD_DIM] = t * jnp.where(j < 2, nrm, 1.0)


def _proj_g(h3, w, cw, tm):
    b, s, d = h3.shape
    c = w.shape[1]
    tn = GROUP_W
    n_pieces = tn // _CONV_COLS
    return pl.pallas_call(
        _proj_g_kernel,
        grid=(c // tn, b, s // tm),
        in_specs=[pl.BlockSpec((None, tm, d), lambda j, bb, i: (bb, i, 0)),
                  pl.BlockSpec((d, tn), lambda j, bb, i: (0, j)),
                  pl.BlockSpec((SUBLANES, tn), lambda j, bb, i: (0, j))],
        out_specs=pl.BlockSpec((1, tm, tn), lambda j, bb, i: (bb, i, j)),
        out_shape=jax.ShapeDtypeStruct((b, s, c), F32),
        scratch_shapes=[pltpu.VMEM((SUBLANES, tn), F32)]
        + [pltpu.VMEM((tm + SUBLANES, _CONV_COLS), F32) for _ in range(n_pieces)],
        compiler_params=_params("arbitrary", "arbitrary", "arbitrary"),
        name="proj_g",
    )(h3, w, cw)


def _gates_kernel(h_ref, w_ref, bias_ref, alog_ref, col_ref, row_ref, carry_scr):
    s = pl.program_id(1)
    y = _dot(h_ref[...], w_ref[...])
    tm = y.shape[0]
    lane = lax.broadcasted_iota(jnp.int32, (1, LANES), 1)
    is_f = lane < N_HEADS
    is_a = (lane >= N_HEADS) & (lane < 2 * N_HEADS)
    yb = y + bias_ref[...]
    sp = _softplus(jnp.where(is_f, -yb, yb))
    val = jnp.where(is_f, -sp, jnp.where(is_a, -jnp.exp(alog_ref[...]) * sp, _sigmoid(y)))

    @pl.when(s == 0)
    def _():
        carry_scr[...] = jnp.zeros_like(carry_scr)

    rowi = lax.broadcasted_iota(jnp.int32, (tm, LANES), 0)
    rowc = rowi % GDN_CHUNK
    cum = val
    cumc = val
    sh = 1
    while sh < tm:
        cum = cum + jnp.where(rowi >= sh, pltpu.roll(cum, sh, 0), 0.0)
        if sh < GDN_CHUNK:
            cumc = cumc + jnp.where(rowc >= sh, pltpu.roll(cumc, sh, 0), 0.0)
        sh *= 2
    cum = cum + carry_scr[0:1, :]
    carry_scr[...] = jnp.broadcast_to(cum[tm - 1:tm, :], carry_scr.shape)
    out = jnp.where(is_f, cum * LOG2E, jnp.where(is_a, cumc, val))
    col_ref[0] = out
    row_ref[0] = out.T[:row_ref.shape[1], :]


def _gates(h3, w, bias, alog, tm):
    b, s, d = h3.shape
    nrow = 32
    return pl.pallas_call(
        _gates_kernel,
        grid=(b, s // tm),
        in_specs=[pl.BlockSpec((None, tm, d), lambda bb, i: (bb, i, 0)),
                  pl.BlockSpec((d, LANES), lambda bb, i: (0, 0)),
                  pl.BlockSpec((1, LANES), lambda bb, i: (0, 0)),
                  pl.BlockSpec((1, LANES), lambda bb, i: (0, 0))],
        out_specs=(pl.BlockSpec((1, tm, LANES), lambda bb, i: (bb, i, 0)),
                   pl.BlockSpec((1, nrow, tm), lambda bb, i: (bb, 0, i))),
        out_shape=(jax.ShapeDtypeStruct((b, s, LANES), F32),
                   jax.ShapeDtypeStruct((b, nrow, s), F32)),
        scratch_shapes=[pltpu.VMEM((SUBLANES, LANES), F32)],
        compiler_params=_params("arbitrary", "arbitrary"),
        name="gates",
    )(h3, w, bias, alog)


def _pick_lane(x, idx):
    lane = lax.broadcasted_iota(jnp.int32, x.shape, 1)
    return jnp.sum(jnp.where(lane == idx, x, 0.0), axis=-1, keepdims=True)


def _attn_kernel(tq, rg, q_ref, k_ref, v_ref, cq_ref, ck_ref, gn_ref, o_ref):
    h = pl.program_id(1)
    qi = pl.program_id(2)
    ng = tq // rg
    cq_all = _pick_lane(cq_ref[0], h)
    qs = [q_ref[0, g * rg:(g + 1) * rg, :] for g in range(ng)]
    cqs = [cq_all[g * rg:(g + 1) * rg] for g in range(ng)]

    def step(off, widths, masked, ms, accs):
        wmax = max(widths)
        k = k_ref[0, pl.ds(off, wmax), :]
        v1 = jnp.concatenate([v_ref[0, pl.ds(off, wmax), :], jnp.ones((wmax, HEAD_DIM), BF16)], axis=1)
        ck = ck_ref[0, 0, :, pl.ds(off, wmax)]
        ss = [_dot_nt(qs[g], k[:widths[g]]) for g in range(ng)]
        new_ms, new_accs = [], []
        for g in range(ng):
            w = widths[g]
            s = ss[g] - ck[:, :w]
            if masked:
                r = lax.broadcasted_iota(jnp.int32, (rg, w), 0) + g * rg
                c = lax.broadcasted_iota(jnp.int32, (rg, w), 1)
                s = jnp.where(c <= r, s, -jnp.inf)
            m_new = jnp.maximum(ms[g], jnp.max(s, axis=-1, keepdims=True) + cqs[g])
            p = jnp.exp2(s - (m_new - cqs[g]))
            new_accs.append(jnp.exp2(ms[g] - m_new) * accs[g] + _dot(p.astype(BF16), v1[:w]))
            new_ms.append(m_new)
        return tuple(new_ms), tuple(new_accs)

    init = (tuple(jnp.full((rg, 1), -jnp.inf, F32) for _ in range(ng)),
            tuple(jnp.zeros((rg, 2 * HEAD_DIM), F32) for _ in range(ng)))
    full = lax.fori_loop(0, qi, lambda j, c: step(pl.multiple_of(j * tq, tq), [tq] * ng, False, *c), init)
    _, accs = step(pl.multiple_of(qi * tq, tq), [(g + 1) * rg for g in range(ng)], True, *full)
    for g in range(ng):
        o = accs[g][:, :HEAD_DIM] / accs[g][:, HEAD_DIM:]
        o_ref[0, g * rg:(g + 1) * rg, :] = (_rms(o) * gn_ref[...]).astype(o_ref.dtype)


def _attn(qkv, col, row, gn, tq, rg):
    b, s, _ = qkv.shape
    return pl.pallas_call(
        functools.partial(_attn_kernel, tq, rg),
        grid=(b, N_HEADS, s // tq),
        in_specs=[pl.BlockSpec((1, tq, HEAD_DIM), lambda bb, h, i: (bb, i, h)),
                  pl.BlockSpec((1, s, HEAD_DIM), lambda bb, h, i: (bb, 0, N_HEADS + h)),
                  pl.BlockSpec((1, s, HEAD_DIM), lambda bb, h, i: (bb, 0, 2 * N_HEADS + h)),
                  pl.BlockSpec((1, tq, LANES), lambda bb, h, i: (bb, i, 0)),
                  pl.BlockSpec((1, 1, 1, s), lambda bb, h, i: (bb, h, 0, 0)),
                  pl.BlockSpec((1, HEAD_DIM), lambda bb, h, i: (0, 0))],
        out_specs=pl.BlockSpec((1, tq, HEAD_DIM), lambda bb, h, i: (bb, i, h)),
        out_shape=jax.ShapeDtypeStruct((b, s, GROUP_W), BF16),
        compiler_params=_params("arbitrary", "arbitrary", "arbitrary"),
        name="attn",
    )(qkv, qkv, qkv, col, row, gn)


def _unit_lower_inverses(lowers):
    c = lowers[0].shape[0]
    r = lax.broadcasted_iota(jnp.int32, (c, c), 0)
    cc = lax.broadcasted_iota(jnp.int32, (c, c), 1)
    eye = jnp.where(r == cc, 1.0, 0.0)
    pair = (r ^ cc) == 1
    invs = [eye - jnp.where(pair, low, 0.0) for low in lowers]
    level = 1
    while (1 << level) < c:
        sel = ((r >> level) ^ (cc >> level)) == 1
        inv_b = [inv.astype(BF16) for inv in invs]
        half = [_dot(ib, jnp.where(sel, low, 0.0).astype(BF16)).astype(BF16) for ib, low in zip(inv_b, lowers)]
        invs = [inv - _dot(hf, ib) for inv, hf, ib in zip(invs, half, inv_b)]
        level += 1
    return invs


def _gdn_kernel(t_tokens, q_ref, k_ref, v_ref, z_ref, col_ref, row_ref, gn_ref, o_ref, state_scr):
    t = pl.program_id(1)
    c = GDN_CHUNK
    n_chunks = t_tokens // c

    @pl.when(t == 0)
    def _():
        state_scr[...] = jnp.zeros_like(state_scr)

    col = col_ref[0]
    r = lax.broadcasted_iota(jnp.int32, (c, c), 0)
    cc = lax.broadcasted_iota(jnp.int32, (c, c), 1)
    incl = cc <= r
    strict = cc < r
    chains = [(hh, ci) for ci in range(n_chunks) for hh in range(N_HEADS)]
    g_all = [_pick_lane(col, N_HEADS + hh) for hh in range(N_HEADS)]
    b_all = [_pick_lane(col, 2 * N_HEADS + hh) for hh in range(N_HEADS)]

    def tile(ref, hh, ci):
        return ref[0, ci * c:(ci + 1) * c, hh * HEAD_DIM:(hh + 1) * HEAD_DIM]

    lowers, a_intra = [], []
    for hh, ci in chains:
        k = tile(k_ref, hh, ci)
        g = g_all[hh][ci * c:(ci + 1) * c]
        g_row = row_ref[0, hh, :, ci * c:(ci + 1) * c]
        decay = jnp.exp(jnp.where(incl, g - g_row, -jnp.inf))
        k_b = k.astype(BF16)
        kb_b = (k * b_all[hh][ci * c:(ci + 1) * c]).astype(BF16)
        lowers.append(jnp.where(strict, _dot_nt(kb_b, k_b) * decay, 0.0))
        a_intra.append(jnp.where(incl, _dot_nt(tile(q_ref, hh, ci).astype(BF16), k_b) * decay, 0.0).astype(BF16))
    t_invs = [inv.astype(BF16) for inv in _unit_lower_inverses(lowers)]
    us, ws = [], []
    for (hh, ci), t_inv in zip(chains, t_invs):
        g = g_all[hh][ci * c:(ci + 1) * c]
        beta = b_all[hh][ci * c:(ci + 1) * c]
        us.append(_dot(t_inv, (tile(v_ref, hh, ci) * beta).astype(BF16)))
        ws.append(_dot(t_inv, (tile(k_ref, hh, ci) * (beta * jnp.exp(g))).astype(BF16)).astype(BF16))

    states = [state_scr[hh] for hh in range(N_HEADS)]
    for idx, (hh, ci) in enumerate(chains):
        g = g_all[hh][ci * c:(ci + 1) * c]
        g_last = g[c - 1:c, :]
        k = tile(k_ref, hh, ci)
        state_b = states[hh].astype(BF16)
        v_new_b = (us[idx] - _dot(ws[idx], state_b)).astype(BF16)
        o = _dot((tile(q_ref, hh, ci) * jnp.exp(g)).astype(BF16), state_b) + _dot(a_intra[idx], v_new_b)
        states[hh] = states[hh] * jnp.exp(g_last) + _dot_tn((k * jnp.exp(g_last - g)).astype(BF16), v_new_b)
        o_ref[0, ci * c:(ci + 1) * c, hh * HEAD_DIM:(hh + 1) * HEAD_DIM] = (
            _rms(o) * gn_ref[...] * tile(z_ref, hh, ci)).astype(o_ref.dtype)
    for hh in range(N_HEADS):
        state_scr[hh] = states[hh]


def _gdn(qkv, z, col, row, gn, t_tokens):
    b, s, _ = qkv.shape
    group = lambda g: pl.BlockSpec((1, t_tokens, GROUP_W), lambda bb, i: (bb, i, g))
    return pl.pallas_call(
        functools.partial(_gdn_kernel, t_tokens),
        grid=(b, s // t_tokens),
        in_specs=[group(0), group(1), group(2), group(0),
                  pl.BlockSpec((1, t_tokens, LANES), lambda bb, i: (bb, i, 0)),
                  pl.BlockSpec((1, N_HEADS, 1, t_tokens), lambda bb, i: (bb, 1, 0, i)),
                  pl.BlockSpec((1, HEAD_DIM), lambda bb, i: (0, 0))],
        out_specs=pl.BlockSpec((1, t_tokens, GROUP_W), lambda bb, i: (bb, i, 0)),
        out_shape=jax.ShapeDtypeStruct((b, s, GROUP_W), BF16),
        scratch_shapes=[pltpu.VMEM((N_HEADS, HEAD_DIM, HEAD_DIM), F32)],
        compiler_params=_params("arbitrary", "arbitrary"),
        name="gdn",
    )(qkv, qkv, qkv, z, col, row, gn)


def _oproj_kernel(of_ref, og_ref, wf_ref, wg_ref, x_ref, gt_ref, o_ref):
    y = _dot(of_ref[...], wf_ref[...]) + _dot(og_ref[...], wg_ref[...])
    o_ref[...] = x_ref[...] + gt_ref[0] * y


def _oproj(o_f, o_g, w_f, w_g, x, mod, tiles_per_batch, tm):
    n, d = x.shape
    row = lambda i: (i, 0)
    return pl.pallas_call(
        _oproj_kernel,
        grid=(n // tm,),
        in_specs=[pl.BlockSpec((tm, GROUP_W), row), pl.BlockSpec((tm, GROUP_W), row),
                  pl.BlockSpec((GROUP_W, d), lambda i: (0, 0)), pl.BlockSpec((GROUP_W, d), lambda i: (0, 0)),
                  pl.BlockSpec((tm, d), row),
                  pl.BlockSpec((1, 1, d), lambda i: ((i // tiles_per_batch) * N_MOD + 5, 0, 0))],
        out_specs=pl.BlockSpec((tm, d), row),
        out_shape=jax.ShapeDtypeStruct((n, d), F32),
        compiler_params=_params("arbitrary"),
        name="oproj",
    )(o_f, o_g, w_f, w_g, x, mod)


def kernel(x, c, ada_w, ada_b, norm_g, ffn_w_gate, ffn_w_up, ffn_w_down, w_in, w_out,
           fox_f_bias, fox_out_norm, gdn_conv, gdn_A_log, gdn_dt_bias, gdn_out_norm, final_norm):
    bsz, s, d = x.shape
    n = bsz * s
    tm = 512
    tiles_per_batch = s // tm
    l = 0
    bf = lambda t: t.astype(BF16)

    c_pad = jnp.zeros((SUBLANES, d), F32).at[:bsz].set(c)
    mod = _ada(c_pad, ada_w[l], ada_b[l][None, :])[:bsz].reshape(bsz * N_MOD, 1, d)

    xf = x.reshape(n, d)
    x1, h2 = _ffn(xf, mod, tiles_per_batch, tm, norm_g[l, 0][None], 0,
                  bf(ffn_w_gate[l, 0]), bf(ffn_w_up[l, 0]), bf(ffn_w_down[l, 0]),
                  norm_g[l, 1][None], 3, final=False)

    w = w_in[l]
    qkv_f = _proj(h2, bf(w[:, _OFF_QF:_OFF_FF]), 1024, False, BF16, "proj_f").reshape(bsz, s, 3 * GROUP_W)
    z_g = _proj(h2, bf(w[:, _OFF_ZG:]), 1024, True, F32, "proj_z").reshape(bsz, s, GROUP_W)
    h3 = h2.reshape(bsz, s, d)
    cw = jnp.zeros((SUBLANES, 3 * GROUP_W), F32).at[:CONV_W].set(gdn_conv[l])
    qkv_g = _proj_g(h3, bf(w[:, _OFF_QKVG:_OFF_AG]), cw, tm)

    pad = LANES - 3 * N_HEADS
    w_s = bf(jnp.concatenate([w[:, _OFF_FF:_OFF_QKVG], w[:, _OFF_AG:_OFF_ZG], jnp.zeros((d, pad), F32)], axis=1))
    zeros8 = jnp.zeros((N_HEADS,), F32)
    bias = jnp.concatenate([fox_f_bias[l], gdn_dt_bias[l], zeros8, jnp.zeros((pad,), F32)])[None]
    alog = jnp.concatenate([zeros8, gdn_A_log[l], zeros8, jnp.zeros((pad,), F32)])[None]
    col, row = _gates(h3, w_s, bias, alog, tm)
    row = row.reshape(bsz, row.shape[1], 1, s)

    o_f = _attn(qkv_f, col, row, fox_out_norm[l][None], min(1024, s), 256)
    o_g = _gdn(qkv_g, z_g, col, row, gdn_out_norm[l][None], 256)

    wo = bf(w_out[l])
    x2 = _oproj(o_f.reshape(n, GROUP_W), o_g.reshape(n, GROUP_W), wo[:GROUP_W], wo[GROUP_W:],
                x1, mod, tiles_per_batch, tm)

    out = _ffn(x2, mod, tiles_per_batch, tm, norm_g[l, 2][None], 6,
               bf(ffn_w_gate[l, 1]), bf(ffn_w_up[l, 1]), bf(ffn_w_down[l, 1]),
               final_norm[None], 0, final=True)
    return out.reshape(bsz, s, d)
```

```python
import functools
import math

import jax
import jax.numpy as jnp
from jax import lax
from jax.experimental import pallas as pl
from jax.experimental.pallas import tpu as pltpu

F32 = jnp.float32
BF16 = jnp.bfloat16

D_MODEL = 2048
HEAD_DIM = 128
N_HEADS = 8
GROUP_W = N_HEADS * HEAD_DIM
D_FF = 5632
CONV_W = 4
N_MOD = 9
EPS = 1e-6
LANES = 128
SUBLANES = 8
GDN_CHUNK = 128
LOG2E = math.log2(math.e)

_OFF_QF = 0
_OFF_FF = 3 * GROUP_W
_OFF_QKVG = _OFF_FF + N_HEADS
_OFF_AG = _OFF_QKVG + 3 * GROUP_W
_OFF_BG = _OFF_AG + N_HEADS
_OFF_ZG = _OFF_BG + N_HEADS

VMEM_LIMIT = 56 * 1024 * 1024


def _params(*sem):
    return pltpu.CompilerParams(dimension_semantics=sem, vmem_limit_bytes=VMEM_LIMIT)


def _sigmoid(x):
    return 1.0 / (1.0 + jnp.exp(-x))


def _silu(x):
    return x * _sigmoid(x)


def _softplus(x):
    return jnp.maximum(x, 0.0) + jnp.log1p(jnp.exp(-jnp.abs(x)))


def _dot(a, b):
    return jnp.dot(a, b, preferred_element_type=F32)


def _dot_nt(a, b):
    return lax.dot_general(a, b, (((1,), (1,)), ((), ())), preferred_element_type=F32)


def _dot_tn(a, b):
    return lax.dot_general(a, b, (((0,), (0,)), ((), ())), preferred_element_type=F32)


def _rms(x):
    return x * lax.rsqrt(jnp.mean(x * x, axis=-1, keepdims=True) + EPS)


def _ada_kernel(c_ref, w_ref, b_ref, o_ref):
    cond = _silu(c_ref[...])
    o_ref[...] = _dot(cond.astype(BF16), w_ref[...].astype(BF16)) + b_ref[...]


def _ada(c_pad, w, b):
    rows, d = c_pad.shape
    n = w.shape[1]
    tn = 1024
    return pl.pallas_call(
        _ada_kernel,
        grid=(n // tn,),
        in_specs=[pl.BlockSpec((rows, d), lambda j: (0, 0)),
                  pl.BlockSpec((d, tn), lambda j: (0, j)),
                  pl.BlockSpec((1, tn), lambda j: (0, j))],
        out_specs=pl.BlockSpec((rows, tn), lambda j: (0, j)),
        out_shape=jax.ShapeDtypeStruct((rows, n), F32),
        compiler_params=_params("arbitrary"),
        name="ada",
    )(c_pad, w, b)


_ROW_CHUNK = 64


def _row_loop(n_rows, body):
    for r in range(n_rows // _ROW_CHUNK):
        body(slice(r * _ROW_CHUNK, (r + 1) * _ROW_CHUNK))


def _inv_rms(x):
    return lax.rsqrt(jnp.mean(x * x, axis=-1, keepdims=True) + EPS)


def _ffn_kernel(final, x_ref, g_ref, sh_ref, sc_ref, gt_ref, wg_ref, wu_ref, wd_ref,
                g2_ref, sh2_ref, sc2_ref, *rest):
    if final:
        o_ref, h_scr, acc_scr, r_scr = rest
    else:
        o_ref, h2_ref, h_scr, acc_scr, r_scr = rest
    f = pl.program_id(1)
    tm = x_ref.shape[0]

    @pl.when(f == 0)
    def _():
        gain = g_ref[...] * (1.0 + sc_ref[0])
        shift = sh_ref[0]

        def stats(rows):
            r_scr[rows, :] = _inv_rms(x_ref[rows, :])
            acc_scr[rows, :] = jnp.zeros((_ROW_CHUNK, acc_scr.shape[1]), F32)

        def modulate(rows):
            h_scr[rows, :] = (x_ref[rows, :] * r_scr[rows, :] * gain + shift).astype(BF16)

        _row_loop(tm, stats)
        _row_loop(tm, modulate)

    h = h_scr[...]
    act = _silu(_dot(h, wg_ref[...])) * _dot(h, wu_ref[...])
    acc_scr[...] += _dot(act.astype(BF16), wd_ref[...])

    @pl.when(f == pl.num_programs(1) - 1)
    def _():
        half_gate = 0.5 * gt_ref[0]
        gain2 = g2_ref[...] if final else g2_ref[...] * (1.0 + sc2_ref[0])

        def residual(rows):
            xn = x_ref[rows, :] + half_gate * acc_scr[rows, :]
            o_ref[rows, :] = xn
            r_scr[rows, :] = _inv_rms(xn)

        def normalise(rows):
            y = o_ref[rows, :] * r_scr[rows, :] * gain2
            if final:
                o_ref[rows, :] = y
            else:
                h2_ref[rows, :] = (y + sh2_ref[0]).astype(BF16)

        _row_loop(tm, residual)
        _row_loop(tm, normalise)


def _ffn(x, mod, tiles_per_batch, tm, g, k_sh, wg, wu, wd, g2, k_sh2, final):
    n, d = x.shape
    dff = wg.shape[1]
    tf = 512
    row = lambda i, f: (i, 0)
    modspec = lambda k: pl.BlockSpec((1, 1, d), lambda i, f: ((i // tiles_per_batch) * N_MOD + k, 0, 0))
    vec = pl.BlockSpec((1, d), lambda i, f: (0, 0))
    in_specs = [pl.BlockSpec((tm, d), row), vec, modspec(k_sh), modspec(k_sh + 1), modspec(k_sh + 2),
                pl.BlockSpec((d, tf), lambda i, f: (0, f)),
                pl.BlockSpec((d, tf), lambda i, f: (0, f)),
                pl.BlockSpec((tf, d), lambda i, f: (f, 0)),
                vec, modspec(k_sh2), modspec(k_sh2 + 1)]
    if final:
        out_shape = jax.ShapeDtypeStruct((n, d), F32)
        out_specs = pl.BlockSpec((tm, d), row)
    else:
        out_shape = (jax.ShapeDtypeStruct((n, d), F32), jax.ShapeDtypeStruct((n, d), BF16))
        out_specs = (pl.BlockSpec((tm, d), row), pl.BlockSpec((tm, d), row))
    return pl.pallas_call(
        functools.partial(_ffn_kernel, final),
        grid=(n // tm, dff // tf),
        in_specs=in_specs,
        out_specs=out_specs,
        out_shape=out_shape,
        scratch_shapes=[pltpu.VMEM((tm, d), BF16), pltpu.VMEM((tm, d), F32), pltpu.VMEM((tm, 1), F32)],
        compiler_params=_params("arbitrary", "arbitrary"),
        name="ffn_final" if final else "ffn",
    )(x, g, mod, mod, mod, wg, wu, wd, g2, mod, mod)


def _proj_kernel(gated, h_ref, w_ref, o_ref):
    y = _dot(h_ref[...], w_ref[...])
    if gated:
        y = _silu(y)
    else:
        y = y * jnp.where(pl.program_id(0) == 0, LOG2E / math.sqrt(HEAD_DIM), 1.0)
    o_ref[...] = y.astype(o_ref.dtype)


def _proj(h, w, tm, gated, out_dtype, name):
    n, d = h.shape
    c = w.shape[1]
    tn = GROUP_W
    return pl.pallas_call(
        functools.partial(_proj_kernel, gated),
        grid=(c // tn, n // tm),
        in_specs=[pl.BlockSpec((tm, d), lambda j, i: (i, 0)),
                  pl.BlockSpec((d, tn), lambda j, i: (0, j))],
        out_specs=pl.BlockSpec((tm, tn), lambda j, i: (i, j)),
        out_shape=jax.ShapeDtypeStruct((n, c), out_dtype),
        compiler_params=_params("arbitrary", "arbitrary"),
        name=name,
    )(h, w)


_CONV_COLS = 2 * HEAD_DIM


def _proj_g_kernel(h_ref, w_ref, cw_ref, o_ref, prev_scr, *bufs):
    j = pl.program_id(0)
    s = pl.program_id(2)
    tm = h_ref.shape[0]
    h = h_ref[...]
    cw = cw_ref[...]
    prev = jnp.where(s == 0, 0.0, prev_scr[...])
    gain = jnp.where(j == 0, HEAD_DIM ** -0.5, 1.0)
    for p, buf in enumerate(bufs):
        cols = slice(p * _CONV_COLS, (p + 1) * _CONV_COLS)
        y = _dot(h, w_ref[:, cols])
        buf[0:SUBLANES, :] = prev[:, cols]
        buf[SUBLANES:, :] = y
        prev_scr[:, cols] = y[tm - SUBLANES:, :]
        out = y * cw[CONV_W - 1:CONV_W, cols]
        for sh in range(1, CONV_W):
            out = out + buf[SUBLANES - sh:SUBLANES - sh + tm, :] * cw[CONV_W - 1 - sh:CONV_W - sh, cols]
        out = _silu(out)
        for hh in range(_CONV_COLS // HEAD_DIM):
            t = out[:, hh * HEAD_DIM:(hh + 1) * HEAD_DIM]
            nrm = lax.rsqrt(jnp.sum(t * t, axis=-1, keepdims=True) + EPS) * gain
            c0 = p * _CONV_COLS + hh * HEAD_DIM
            o_ref[0, :, c0:c0 + HEAD_DIM] = t * jnp.where(j < 2, nrm, 1.0)


def _proj_g(h3, w, cw, tm):
    b, s, d = h3.shape
    c = w.shape[1]
    tn = GROUP_W
    n_pieces = tn // _CONV_COLS
    return pl.pallas_call(
        _proj_g_kernel,
        grid=(c // tn, b, s // tm),
        in_specs=[pl.BlockSpec((None, tm, d), lambda j, bb, i: (bb, i, 0)),
                  pl.BlockSpec((d, tn), lambda j, bb, i: (0, j)),
                  pl.BlockSpec((SUBLANES, tn), lambda j, bb, i: (0, j))],
        out_specs=pl.BlockSpec((1, tm, tn), lambda j, bb, i: (bb, i, j)),
        out_shape=jax.ShapeDtypeStruct((b, s, c), F32),
        scratch_shapes=[pltpu.VMEM((SUBLANES, tn), F32)]
        + [pltpu.VMEM((tm + SUBLANES, _CONV_COLS), F32) for _ in range(n_pieces)],
        compiler_params=_params("arbitrary", "arbitrary", "arbitrary"),
        name="proj_g",
    )(h3, w, cw)


def _gates_kernel(h_ref, w_ref, bias_ref, alog_ref, col_ref, row_ref, carry_scr):
    s = pl.program_id(1)
    y = _dot(h_ref[...], w_ref[...])
    tm = y.shape[0]
    lane = lax.broadcasted_iota(jnp.int32, (1, LANES), 1)
    is_f = lane < N_HEADS
    is_a = (lane >= N_HEADS) & (lane < 2 * N_HEADS)
    yb = y + bias_ref[...]
    sp = _softplus(jnp.where(is_f, -yb, yb))
    val = jnp.where(is_f, -sp, jnp.where(is_a, -jnp.exp(alog_ref[...]) * sp, _sigmoid(y)))

    @pl.when(s == 0)
    def _():
        carry_scr[...] = jnp.zeros_like(carry_scr)

    rowi = lax.broadcasted_iota(jnp.int32, (tm, LANES), 0)
    rowc = rowi % GDN_CHUNK
    cum = val
    cumc = val
    sh = 1
    while sh < tm:
        cum = cum + jnp.where(rowi >= sh, pltpu.roll(cum, sh, 0), 0.0)
        if sh < GDN_CHUNK:
            cumc = cumc + jnp.where(rowc >= sh, pltpu.roll(cumc, sh, 0), 0.0)
        sh *= 2
    cum = cum + carry_scr[0:1, :]
    carry_scr[...] = jnp.broadcast_to(cum[tm - 1:tm, :], carry_scr.shape)
    out = jnp.where(is_f, cum * LOG2E, jnp.where(is_a, cumc, val))
    col_ref[0] = out
    row_ref[0] = out.T[:row_ref.shape[1], :]


def _gates(h3, w, bias, alog, tm):
    b, s, d = h3.shape
    nrow = 32
    return pl.pallas_call(
        _gates_kernel,
        grid=(b, s // tm),
        in_specs=[pl.BlockSpec((None, tm, d), lambda bb, i: (bb, i, 0)),
                  pl.BlockSpec((d, LANES), lambda bb, i: (0, 0)),
                  pl.BlockSpec((1, LANES), lambda bb, i: (0, 0)),
                  pl.BlockSpec((1, LANES), lambda bb, i: (0, 0))],
        out_specs=(pl.BlockSpec((1, tm, LANES), lambda bb, i: (bb, i, 0)),
                   pl.BlockSpec((1, nrow, tm), lambda bb, i: (bb, 0, i))),
        out_shape=(jax.ShapeDtypeStruct((b, s, LANES), F32),
                   jax.ShapeDtypeStruct((b, nrow, s), F32)),
        scratch_shapes=[pltpu.VMEM((SUBLANES, LANES), F32)],
        compiler_params=_params("arbitrary", "arbitrary"),
        name="gates",
    )(h3, w, bias, alog)


def _pick_lane(x, idx):
    lane = lax.broadcasted_iota(jnp.int32, x.shape, 1)
    return jnp.sum(jnp.where(lane == idx, x, 0.0), axis=-1, keepdims=True)


def _attn_kernel(tq, rg, tk, q_ref, k_ref, v_ref, cq_ref, ck_ref, gn_ref, o_ref, m_scr, acc_scr):
    h = pl.program_id(1)
    qi = pl.program_id(2)
    ng = tq // rg
    ahead = 2
    cq_all = _pick_lane(cq_ref[0], h)
    qs = [q_ref[0, g * rg:(g + 1) * rg, :] for g in range(ng)]
    cqs = [cq_all[g * rg:(g + 1) * rg] for g in range(ng)]
    q0 = pl.multiple_of(qi * tq, tq)

    def step(off, widths, first):
        wmax = max(widths)
        k = k_ref[0, pl.ds(off, wmax), :]
        v1 = jnp.concatenate([v_ref[0, pl.ds(off, wmax), :], jnp.ones((wmax, HEAD_DIM), BF16)], axis=1)
        ck = ck_ref[0, 0, :, pl.ds(off, wmax)]
        ss = {g: _dot_nt(qs[g], k[:widths[g]]) for g in range(min(ahead, ng))}
        for g in range(ng):
            w = widths[g]
            rows = slice(g * rg, (g + 1) * rg)
            if g + ahead < ng:
                ss[g + ahead] = _dot_nt(qs[g + ahead], k[:widths[g + ahead]])
            s = ss.pop(g) - ck[:, :w]
            if first:
                r = lax.broadcasted_iota(jnp.int32, (rg, w), 0) + g * rg
                c = lax.broadcasted_iota(jnp.int32, (rg, w), 1)
                s = jnp.where(c <= r, s, -jnp.inf)
                m_new = jnp.max(s, axis=-1, keepdims=True) + cqs[g]
            else:
                m_old = m_scr[rows, :]
                m_new = jnp.maximum(m_old, jnp.max(s, axis=-1, keepdims=True) + cqs[g])
            p = jnp.exp2(s - (m_new - cqs[g]))
            pv = _dot(p.astype(BF16), v1[:w])
            acc_scr[rows, :] = pv if first else jnp.exp2(m_old - m_new) * acc_scr[rows, :] + pv
            m_scr[rows, :] = m_new

    step(q0, [(g + 1) * rg for g in range(ng)], True)

    def body(j, carry):
        step(pl.multiple_of(j * tk, tk), [tk] * ng, False)
        return carry

    lax.fori_loop(0, qi * (tq // tk), body, 0)
    for g in range(ng):
        rows = slice(g * rg, (g + 1) * rg)
        o = acc_scr[rows, :HEAD_DIM] / acc_scr[rows, HEAD_DIM:]
        o_ref[0, rows, :] = (_rms(o) * gn_ref[...]).astype(o_ref.dtype)


def _attn(qkv, col, row, gn, tq, rg, tk):
    b, s, _ = qkv.shape
    return pl.pallas_call(
        functools.partial(_attn_kernel, tq, rg, tk),
        grid=(b, N_HEADS, s // tq),
        in_specs=[pl.BlockSpec((1, tq, HEAD_DIM), lambda bb, h, i: (bb, i, h)),
                  pl.BlockSpec((1, s, HEAD_DIM), lambda bb, h, i: (bb, 0, N_HEADS + h)),
                  pl.BlockSpec((1, s, HEAD_DIM), lambda bb, h, i: (bb, 0, 2 * N_HEADS + h)),
                  pl.BlockSpec((1, tq, LANES), lambda bb, h, i: (bb, i, 0)),
                  pl.BlockSpec((1, 1, 1, s), lambda bb, h, i: (bb, h, 0, 0)),
                  pl.BlockSpec((1, HEAD_DIM), lambda bb, h, i: (0, 0))],
        out_specs=pl.BlockSpec((1, tq, HEAD_DIM), lambda bb, h, i: (bb, i, h)),
        out_shape=jax.ShapeDtypeStruct((b, s, GROUP_W), BF16),
        scratch_shapes=[pltpu.VMEM((tq, 1), F32), pltpu.VMEM((tq, 2 * HEAD_DIM), F32)],
        compiler_params=_params("arbitrary", "arbitrary", "arbitrary"),
        name="attn",
    )(qkv, qkv, qkv, col, row, gn)


def _unit_lower_inverses(lowers):
    c = lowers[0].shape[0]
    r = lax.broadcasted_iota(jnp.int32, (c, c), 0)
    cc = lax.broadcasted_iota(jnp.int32, (c, c), 1)
    eye = jnp.where(r == cc, 1.0, 0.0)
    pair = (r ^ cc) == 1
    invs = [eye - jnp.where(pair, low, 0.0) for low in lowers]
    level = 1
    while (1 << level) < c:
        sel = ((r >> level) ^ (cc >> level)) == 1
        inv_b = [inv.astype(BF16) for inv in invs]
        half = [_dot(ib, jnp.where(sel, low, 0.0).astype(BF16)).astype(BF16) for ib, low in zip(inv_b, lowers)]
        invs = [inv - _dot(hf, ib) for inv, hf, ib in zip(invs, half, inv_b)]
        level += 1
    return invs


def _gdn_kernel(t_tokens, q_ref, k_ref, v_ref, z_ref, col_ref, row_ref, gn_ref, o_ref, state_scr):
    t = pl.program_id(1)
    c = GDN_CHUNK
    n_chunks = t_tokens // c

    @pl.when(t == 0)
    def _():
        state_scr[...] = jnp.zeros_like(state_scr)

    col = col_ref[0]
    r = lax.broadcasted_iota(jnp.int32, (c, c), 0)
    cc = lax.broadcasted_iota(jnp.int32, (c, c), 1)
    incl = cc <= r
    strict = cc < r
    chains = [(hh, ci) for ci in range(n_chunks) for hh in range(N_HEADS)]
    g_all = [_pick_lane(col, N_HEADS + hh) for hh in range(N_HEADS)]
    b_all = [_pick_lane(col, 2 * N_HEADS + hh) for hh in range(N_HEADS)]

    def tile(ref, hh, ci):
        return ref[0, ci * c:(ci + 1) * c, hh * HEAD_DIM:(hh + 1) * HEAD_DIM]

    lowers, a_intra = [], []
    for hh, ci in chains:
        k = tile(k_ref, hh, ci)
        g = g_all[hh][ci * c:(ci + 1) * c]
        g_row = row_ref[0, hh, :, ci * c:(ci + 1) * c]
        decay = jnp.exp(jnp.where(incl, g - g_row, -jnp.inf))
        k_b = k.astype(BF16)
        kb_b = (k * b_all[hh][ci * c:(ci + 1) * c]).astype(BF16)
        lowers.append(jnp.where(strict, _dot_nt(kb_b, k_b) * decay, 0.0))
        a_intra.append(jnp.where(incl, _dot_nt(tile(q_ref, hh, ci).astype(BF16), k_b) * decay, 0.0).astype(BF16))
    t_invs = [inv.astype(BF16) for inv in _unit_lower_inverses(lowers)]
    us, ws = [], []
    for (hh, ci), t_inv in zip(chains, t_invs):
        g = g_all[hh][ci * c:(ci + 1) * c]
        beta = b_all[hh][ci * c:(ci + 1) * c]
        us.append(_dot(t_inv, (tile(v_ref, hh, ci) * beta).astype(BF16)))
        ws.append(_dot(t_inv, (tile(k_ref, hh, ci) * (beta * jnp.exp(g))).astype(BF16)).astype(BF16))

    states = [state_scr[hh] for hh in range(N_HEADS)]
    for idx, (hh, ci) in enumerate(chains):
        g = g_all[hh][ci * c:(ci + 1) * c]
        g_last = g[c - 1:c, :]
        k = tile(k_ref, hh, ci)
        state_b = states[hh].astype(BF16)
        v_new_b = (us[idx] - _dot(ws[idx], state_b)).astype(BF16)
        o = _dot((tile(q_ref, hh, ci) * jnp.exp(g)).astype(BF16), state_b) + _dot(a_intra[idx], v_new_b)
        states[hh] = states[hh] * jnp.exp(g_last) + _dot_tn((k * jnp.exp(g_last - g)).astype(BF16), v_new_b)
        o_ref[0, ci * c:(ci + 1) * c, hh * HEAD_DIM:(hh + 1) * HEAD_DIM] = (
            _rms(o) * gn_ref[...] * tile(z_ref, hh, ci)).astype(o_ref.dtype)
    for hh in range(N_HEADS):
        state_scr[hh] = states[hh]


def _gdn(qkv, z, col, row, gn, t_tokens):
    b, s, _ = qkv.shape
    group = lambda g: pl.BlockSpec((1, t_tokens, GROUP_W), lambda bb, i: (bb, i, g))
    return pl.pallas_call(
        functools.partial(_gdn_kernel, t_tokens),
        grid=(b, s // t_tokens),
        in_specs=[group(0), group(1), group(2), group(0),
                  pl.BlockSpec((1, t_tokens, LANES), lambda bb, i: (bb, i, 0)),
                  pl.BlockSpec((1, N_HEADS, 1, t_tokens), lambda bb, i: (bb, 1, 0, i)),
                  pl.BlockSpec((1, HEAD_DIM), lambda bb, i: (0, 0))],
        out_specs=pl.BlockSpec((1, t_tokens, GROUP_W), lambda bb, i: (bb, i, 0)),
        out_shape=jax.ShapeDtypeStruct((b, s, GROUP_W), BF16),
        scratch_shapes=[pltpu.VMEM((N_HEADS, HEAD_DIM, HEAD_DIM), F32)],
        compiler_params=_params("arbitrary", "arbitrary"),
        name="gdn",
    )(qkv, qkv, qkv, z, col, row, gn)


def _oproj_kernel(of_ref, og_ref, wf_ref, wg_ref, x_ref, gt_ref, o_ref):
    y = _dot(of_ref[...], wf_ref[...]) + _dot(og_ref[...], wg_ref[...])
    o_ref[...] = x_ref[...] + gt_ref[0] * y


def _oproj(o_f, o_g, w_f, w_g, x, mod, tiles_per_batch, tm):
    n, d = x.shape
    row = lambda i: (i, 0)
    return pl.pallas_call(
        _oproj_kernel,
        grid=(n // tm,),
        in_specs=[pl.BlockSpec((tm, GROUP_W), row), pl.BlockSpec((tm, GROUP_W), row),
                  pl.BlockSpec((GROUP_W, d), lambda i: (0, 0)), pl.BlockSpec((GROUP_W, d), lambda i: (0, 0)),
                  pl.BlockSpec((tm, d), row),
                  pl.BlockSpec((1, 1, d), lambda i: ((i // tiles_per_batch) * N_MOD + 5, 0, 0))],
        out_specs=pl.BlockSpec((tm, d), row),
        out_shape=jax.ShapeDtypeStruct((n, d), F32),
        compiler_params=_params("arbitrary"),
        name="oproj",
    )(o_f, o_g, w_f, w_g, x, mod)


def kernel(x, c, ada_w, ada_b, norm_g, ffn_w_gate, ffn_w_up, ffn_w_down, w_in, w_out,
           fox_f_bias, fox_out_norm, gdn_conv, gdn_A_log, gdn_dt_bias, gdn_out_norm, final_norm):
    bsz, s, d = x.shape
    n = bsz * s
    tm = 512
    tiles_per_batch = s // tm
    l = 0
    bf = lambda t: t.astype(BF16)

    c_pad = jnp.zeros((SUBLANES, d), F32).at[:bsz].set(c)
    mod = _ada(c_pad, ada_w[l], ada_b[l][None, :])[:bsz].reshape(bsz * N_MOD, 1, d)

    xf = x.reshape(n, d)
    x1, h2 = _ffn(xf, mod, tiles_per_batch, tm, norm_g[l, 0][None], 0,
                  bf(ffn_w_gate[l, 0]), bf(ffn_w_up[l, 0]), bf(ffn_w_down[l, 0]),
                  norm_g[l, 1][None], 3, final=False)

    w = w_in[l]
    qkv_f = _proj(h2, bf(w[:, _OFF_QF:_OFF_FF]), 1024, False, BF16, "proj_f").reshape(bsz, s, 3 * GROUP_W)
    z_g = _proj(h2, bf(w[:, _OFF_ZG:]), 1024, True, F32, "proj_z").reshape(bsz, s, GROUP_W)
    h3 = h2.reshape(bsz, s, d)
    cw = jnp.zeros((SUBLANES, 3 * GROUP_W), F32).at[:CONV_W].set(gdn_conv[l])
    qkv_g = _proj_g(h3, bf(w[:, _OFF_QKVG:_OFF_AG]), cw, tm)

    pad = LANES - 3 * N_HEADS
    w_s = bf(jnp.concatenate([w[:, _OFF_FF:_OFF_QKVG], w[:, _OFF_AG:_OFF_ZG], jnp.zeros((d, pad), F32)], axis=1))
    zeros8 = jnp.zeros((N_HEADS,), F32)
    bias = jnp.concatenate([fox_f_bias[l], gdn_dt_bias[l], zeros8, jnp.zeros((pad,), F32)])[None]
    alog = jnp.concatenate([zeros8, gdn_A_log[l], zeros8, jnp.zeros((pad,), F32)])[None]
    col, row = _gates(h3, w_s, bias, alog, tm)
    row = row.reshape(bsz, row.shape[1], 1, s)

    o_f = _attn(qkv_f, col, row, fox_out_norm[l][None], min(2048, s), 256, 1024)
    o_g = _gdn(qkv_g, z_g, col, row, gdn_out_norm[l][None], 256)

    wo = bf(w_out[l])
    x2 = _oproj(o_f.reshape(n, GROUP_W), o_g.reshape(n, GROUP_W), wo[:GROUP_W], wo[GROUP_W:],
                x1, mod, tiles_per_batch, tm)

    out = _ffn(x2, mod, tiles_per_batch, tm, norm_g[l, 2][None], 6,
               bf(ffn_w_gate[l, 1]), bf(ffn_w_up[l, 1]), bf(ffn_w_down[l, 1]),
               final_norm[None], 0, final=True)
    return out.reshape(bsz, s, d)
```

```python
import functools
import math

import jax
import jax.numpy as jnp
from jax import lax
from jax.experimental import pallas as pl
from jax.experimental.pallas import tpu as pltpu

F32 = jnp.float32
BF16 = jnp.bfloat16

D_MODEL = 2048
HEAD_DIM = 128
N_HEADS = 8
GROUP_W = N_HEADS * HEAD_DIM
D_FF = 5632
CONV_W = 4
N_MOD = 9
EPS = 1e-6
LANES = 128
SUBLANES = 8
GDN_CHUNK = 128
LOG2E = math.log2(math.e)

_OFF_QF = 0
_OFF_FF = 3 * GROUP_W
_OFF_QKVG = _OFF_FF + N_HEADS
_OFF_AG = _OFF_QKVG + 3 * GROUP_W
_OFF_BG = _OFF_AG + N_HEADS
_OFF_ZG = _OFF_BG + N_HEADS

VMEM_LIMIT = 56 * 1024 * 1024


def _params(*sem):
    return pltpu.CompilerParams(dimension_semantics=sem, vmem_limit_bytes=VMEM_LIMIT)


def _sigmoid(x):
    return 1.0 / (1.0 + jnp.exp(-x))


def _silu(x):
    return x * _sigmoid(x)


def _softplus(x):
    return jnp.maximum(x, 0.0) + jnp.log1p(jnp.exp(-jnp.abs(x)))


def _dot(a, b):
    return jnp.dot(a, b, preferred_element_type=F32)


def _dot_nt(a, b):
    return lax.dot_general(a, b, (((1,), (1,)), ((), ())), preferred_element_type=F32)


def _dot_tn(a, b):
    return lax.dot_general(a, b, (((0,), (0,)), ((), ())), preferred_element_type=F32)


def _rms(x):
    return x * lax.rsqrt(jnp.mean(x * x, axis=-1, keepdims=True) + EPS)


def _ada_kernel(c_ref, w_ref, b_ref, o_ref):
    cond = _silu(c_ref[...])
    o_ref[...] = _dot(cond.astype(BF16), w_ref[...].astype(BF16)) + b_ref[...]


def _ada(c_pad, w, b, layer):
    rows, d = c_pad.shape
    n = w.shape[2]
    tn = 1024
    return pl.pallas_call(
        _ada_kernel,
        grid=(n // tn,),
        in_specs=[pl.BlockSpec((rows, d), lambda j: (0, 0)),
                  pl.BlockSpec((None, d, tn), lambda j: (layer, 0, j)),
                  pl.BlockSpec((1, tn), lambda j: (layer, j))],
        out_specs=pl.BlockSpec((rows, tn), lambda j: (0, j)),
        out_shape=jax.ShapeDtypeStruct((rows, n), F32),
        compiler_params=_params("arbitrary"),
        name="ada",
    )(c_pad, w, b)


_ROW_CHUNK = 64


def _row_loop(n_rows, body):
    for r in range(n_rows // _ROW_CHUNK):
        body(slice(r * _ROW_CHUNK, (r + 1) * _ROW_CHUNK))


def _inv_rms(x):
    return lax.rsqrt(jnp.mean(x * x, axis=-1, keepdims=True) + EPS)


def _ffn_kernel(final, x_ref, g_ref, sh_ref, sc_ref, gt_ref, wg_ref, wu_ref, wd_ref,
                g2_ref, sh2_ref, sc2_ref, *rest):
    if final:
        o_ref, h_scr, acc_scr, r_scr = rest
    else:
        o_ref, h2_ref, h_scr, acc_scr, r_scr = rest
    f = pl.program_id(1)
    tm = x_ref.shape[0]

    @pl.when(f == 0)
    def _():
        gain = g_ref[...] * (1.0 + sc_ref[0])
        shift = sh_ref[0]

        def stats(rows):
            r_scr[rows, :] = _inv_rms(x_ref[rows, :])
            acc_scr[rows, :] = jnp.zeros((_ROW_CHUNK, acc_scr.shape[1]), F32)

        def modulate(rows):
            h_scr[rows, :] = (x_ref[rows, :] * r_scr[rows, :] * gain + shift).astype(BF16)

        _row_loop(tm, stats)
        _row_loop(tm, modulate)

    h = h_scr[...]
    act = _silu(_dot(h, wg_ref[...])) * _dot(h, wu_ref[...])
    acc_scr[...] += _dot(act.astype(BF16), wd_ref[...])

    @pl.when(f == pl.num_programs(1) - 1)
    def _():
        half_gate = 0.5 * gt_ref[0]
        gain2 = g2_ref[...] if final else g2_ref[...] * (1.0 + sc2_ref[0])

        def residual(rows):
            xn = x_ref[rows, :] + half_gate * acc_scr[rows, :]
            o_ref[rows, :] = xn
            r_scr[rows, :] = _inv_rms(xn)

        def normalise(rows):
            y = o_ref[rows, :] * r_scr[rows, :] * gain2
            if final:
                o_ref[rows, :] = y
            else:
                h2_ref[rows, :] = (y + sh2_ref[0]).astype(BF16)

        _row_loop(tm, residual)
        _row_loop(tm, normalise)


def _ffn(x, mod, tiles_per_batch, tm, g, k_sh, wg, wu, wd, g2, k_sh2, final):
    n, d = x.shape
    dff = wg.shape[1]
    tf = 512
    row = lambda i, f: (i, 0)
    modspec = lambda k: pl.BlockSpec((1, 1, d), lambda i, f: ((i // tiles_per_batch) * N_MOD + k, 0, 0))
    vec = pl.BlockSpec((1, d), lambda i, f: (0, 0))
    in_specs = [pl.BlockSpec((tm, d), row), vec, modspec(k_sh), modspec(k_sh + 1), modspec(k_sh + 2),
                pl.BlockSpec((d, tf), lambda i, f: (0, f)),
                pl.BlockSpec((d, tf), lambda i, f: (0, f)),
                pl.BlockSpec((tf, d), lambda i, f: (f, 0)),
                vec, modspec(k_sh2), modspec(k_sh2 + 1)]
    if final:
        out_shape = jax.ShapeDtypeStruct((n, d), F32)
        out_specs = pl.BlockSpec((tm, d), row)
    else:
        out_shape = (jax.ShapeDtypeStruct((n, d), F32), jax.ShapeDtypeStruct((n, d), BF16))
        out_specs = (pl.BlockSpec((tm, d), row), pl.BlockSpec((tm, d), row))
    return pl.pallas_call(
        functools.partial(_ffn_kernel, final),
        grid=(n // tm, dff // tf),
        in_specs=in_specs,
        out_specs=out_specs,
        out_shape=out_shape,
        scratch_shapes=[pltpu.VMEM((tm, d), BF16), pltpu.VMEM((tm, d), F32), pltpu.VMEM((tm, 1), F32)],
        compiler_params=_params("arbitrary", "arbitrary"),
        name="ffn_final" if final else "ffn",
    )(x, g, mod, mod, mod, wg, wu, wd, g2, mod, mod)


def _proj_kernel(gated, h_ref, w_ref, o_ref):
    y = _dot(h_ref[...], w_ref[...])
    if gated:
        y = _silu(y)
    else:
        y = y * jnp.where(pl.program_id(0) == 0, LOG2E / math.sqrt(HEAD_DIM), 1.0)
    o_ref[...] = y.astype(o_ref.dtype)


def _proj(h, w, tm, gated, out_dtype, name):
    n, d = h.shape
    c = w.shape[1]
    tn = GROUP_W
    return pl.pallas_call(
        functools.partial(_proj_kernel, gated),
        grid=(c // tn, n // tm),
        in_specs=[pl.BlockSpec((tm, d), lambda j, i: (i, 0)),
                  pl.BlockSpec((d, tn), lambda j, i: (0, j))],
        out_specs=pl.BlockSpec((tm, tn), lambda j, i: (i, j)),
        out_shape=jax.ShapeDtypeStruct((n, c), out_dtype),
        compiler_params=_params("arbitrary", "arbitrary"),
        name=name,
    )(h, w)


_CONV_COLS = 2 * HEAD_DIM


def _proj_g_kernel(h_ref, w_ref, cw_ref, o_ref, prev_scr, *bufs):
    j = pl.program_id(0)
    s = pl.program_id(2)
    tm = h_ref.shape[0]
    h = h_ref[...]
    cw = cw_ref[...]
    prev = jnp.where(s == 0, 0.0, prev_scr[...])
    gain = jnp.where(j == 0, HEAD_DIM ** -0.5, 1.0)
    for p, buf in enumerate(bufs):
        cols = slice(p * _CONV_COLS, (p + 1) * _CONV_COLS)
        y = _dot(h, w_ref[:, cols])
        buf[0:SUBLANES, :] = prev[:, cols]
        buf[SUBLANES:, :] = y
        prev_scr[:, cols] = y[tm - SUBLANES:, :]
        out = y * cw[CONV_W - 1:CONV_W, cols]
        for sh in range(1, CONV_W):
            out = out + buf[SUBLANES - sh:SUBLANES - sh + tm, :] * cw[CONV_W - 1 - sh:CONV_W - sh, cols]
        out = _silu(out)
        for hh in range(_CONV_COLS // HEAD_DIM):
            t = out[:, hh * HEAD_DIM:(hh + 1) * HEAD_DIM]
            nrm = lax.rsqrt(jnp.sum(t * t, axis=-1, keepdims=True) + EPS) * gain
            c0 = p * _CONV_COLS + hh * HEAD_DIM
            o_ref[0, :, c0:c0 + HEAD_DIM] = t * jnp.where(j < 2, nrm, 1.0)


def _proj_g(h3, w, cw, tm):
    b, s, d = h3.shape
    c = w.shape[1]
    tn = GROUP_W
    n_pieces = tn // _CONV_COLS
    return pl.pallas_call(
        _proj_g_kernel,
        grid=(c // tn, b, s // tm),
        in_specs=[pl.BlockSpec((None, tm, d), lambda j, bb, i: (bb, i, 0)),
                  pl.BlockSpec((d, tn), lambda j, bb, i: (0, j)),
                  pl.BlockSpec((SUBLANES, tn), lambda j, bb, i: (0, j))],
        out_specs=pl.BlockSpec((1, tm, tn), lambda j, bb, i: (bb, i, j)),
        out_shape=jax.ShapeDtypeStruct((b, s, c), F32),
        scratch_shapes=[pltpu.VMEM((SUBLANES, tn), F32)]
        + [pltpu.VMEM((tm + SUBLANES, _CONV_COLS), F32) for _ in range(n_pieces)],
        compiler_params=_params("arbitrary", "arbitrary", "arbitrary"),
        name="proj_g",
    )(h3, w, cw)


def _gates_kernel(h_ref, w_ref, bias_ref, alog_ref, col_ref, row_ref, carry_scr):
    s = pl.program_id(1)
    y = _dot(h_ref[...], w_ref[...])
    tm = y.shape[0]
    lane = lax.broadcasted_iota(jnp.int32, (1, LANES), 1)
    is_f = lane < N_HEADS
    is_a = (lane >= N_HEADS) & (lane < 2 * N_HEADS)
    yb = y + bias_ref[...]
    sp = _softplus(jnp.where(is_f, -yb, yb))
    val = jnp.where(is_f, -sp, jnp.where(is_a, -jnp.exp(alog_ref[...]) * sp, _sigmoid(y)))

    @pl.when(s == 0)
    def _():
        carry_scr[...] = jnp.zeros_like(carry_scr)

    rowi = lax.broadcasted_iota(jnp.int32, (tm, LANES), 0)
    rowc = rowi % GDN_CHUNK
    cum = val
    cumc = val
    sh = 1
    while sh < tm:
        cum = cum + jnp.where(rowi >= sh, pltpu.roll(cum, sh, 0), 0.0)
        if sh < GDN_CHUNK:
            cumc = cumc + jnp.where(rowc >= sh, pltpu.roll(cumc, sh, 0), 0.0)
        sh *= 2
    cum = cum + carry_scr[0:1, :]
    carry_scr[...] = jnp.broadcast_to(cum[tm - 1:tm, :], carry_scr.shape)
    out = jnp.where(is_f, cum * LOG2E, jnp.where(is_a, cumc, val))
    col_ref[0] = out
    row_ref[0] = out.T[:row_ref.shape[1], :]


def _gates(h3, w, bias, alog, tm):
    b, s, d = h3.shape
    nrow = 32
    return pl.pallas_call(
        _gates_kernel,
        grid=(b, s // tm),
        in_specs=[pl.BlockSpec((None, tm, d), lambda bb, i: (bb, i, 0)),
                  pl.BlockSpec((d, LANES), lambda bb, i: (0, 0)),
                  pl.BlockSpec((1, LANES), lambda bb, i: (0, 0)),
                  pl.BlockSpec((1, LANES), lambda bb, i: (0, 0))],
        out_specs=(pl.BlockSpec((1, tm, LANES), lambda bb, i: (bb, i, 0)),
                   pl.BlockSpec((1, nrow, tm), lambda bb, i: (bb, 0, i))),
        out_shape=(jax.ShapeDtypeStruct((b, s, LANES), F32),
                   jax.ShapeDtypeStruct((b, nrow, s), F32)),
        scratch_shapes=[pltpu.VMEM((SUBLANES, LANES), F32)],
        compiler_params=_params("arbitrary", "arbitrary"),
        name="gates",
    )(h3, w, bias, alog)


def _pick_lane(x, idx):
    lane = lax.broadcasted_iota(jnp.int32, x.shape, 1)
    return jnp.sum(jnp.where(lane == idx, x, 0.0), axis=-1, keepdims=True)


def _attn_kernel(tq, rg, tk, q_ref, k_ref, v_ref, cq_ref, ck_ref, gn_ref, o_ref, m_scr, acc_scr):
    h = pl.program_id(1)
    qi = pl.program_id(2)
    ng = tq // rg
    ahead = 3
    cq_all = _pick_lane(cq_ref[0], h)
    qs = [q_ref[0, g * rg:(g + 1) * rg, :] for g in range(ng)]
    cqs = [cq_all[g * rg:(g + 1) * rg] for g in range(ng)]
    q0 = pl.multiple_of(qi * tq, tq)

    def step(off, widths, first):
        wmax = max(widths)
        k = k_ref[0, pl.ds(off, wmax), :]
        v1 = jnp.concatenate([v_ref[0, pl.ds(off, wmax), :], jnp.ones((wmax, HEAD_DIM), BF16)], axis=1)
        ck = ck_ref[0, 0, :, pl.ds(off, wmax)]
        ss = {g: _dot_nt(qs[g], k[:widths[g]]) for g in range(min(ahead, ng))}
        for g in range(ng):
            w = widths[g]
            rows = slice(g * rg, (g + 1) * rg)
            if g + ahead < ng:
                ss[g + ahead] = _dot_nt(qs[g + ahead], k[:widths[g + ahead]])
            s = ss.pop(g) - ck[:, :w]
            if first:
                r = lax.broadcasted_iota(jnp.int32, (rg, w), 0) + g * rg
                c = lax.broadcasted_iota(jnp.int32, (rg, w), 1)
                s = jnp.where(c <= r, s, -jnp.inf)
                m_new = jnp.max(s, axis=-1, keepdims=True) + cqs[g]
            else:
                m_old = m_scr[rows, :]
                m_new = jnp.maximum(m_old, jnp.max(s, axis=-1, keepdims=True) + cqs[g])
            p = jnp.exp2(s - (m_new - cqs[g]))
            pv = _dot(p.astype(BF16), v1[:w])
            acc_scr[rows, :] = pv if first else jnp.exp2(m_old - m_new) * acc_scr[rows, :] + pv
            m_scr[rows, :] = m_new

    step(q0, [(g + 1) * rg for g in range(ng)], True)

    def body(j, carry):
        step(pl.multiple_of(j * tk, tk), [tk] * ng, False)
        return carry

    lax.fori_loop(0, qi * (tq // tk), body, 0)
    for g in range(ng):
        rows = slice(g * rg, (g + 1) * rg)
        o = acc_scr[rows, :HEAD_DIM] / acc_scr[rows, HEAD_DIM:]
        o_ref[0, rows, :] = (_rms(o) * gn_ref[...]).astype(o_ref.dtype)


def _attn(qkv, col, row, gn, tq, rg, tk):
    b, s, _ = qkv.shape
    return pl.pallas_call(
        functools.partial(_attn_kernel, tq, rg, tk),
        grid=(b, N_HEADS, s // tq),
        in_specs=[pl.BlockSpec((1, tq, HEAD_DIM), lambda bb, h, i: (bb, i, h)),
                  pl.BlockSpec((1, s, HEAD_DIM), lambda bb, h, i: (bb, 0, N_HEADS + h)),
                  pl.BlockSpec((1, s, HEAD_DIM), lambda bb, h, i: (bb, 0, 2 * N_HEADS + h)),
                  pl.BlockSpec((1, tq, LANES), lambda bb, h, i: (bb, i, 0)),
                  pl.BlockSpec((1, 1, 1, s), lambda bb, h, i: (bb, h, 0, 0)),
                  pl.BlockSpec((1, HEAD_DIM), lambda bb, h, i: (0, 0))],
        out_specs=pl.BlockSpec((1, tq, HEAD_DIM), lambda bb, h, i: (bb, i, h)),
        out_shape=jax.ShapeDtypeStruct((b, s, GROUP_W), BF16),
        scratch_shapes=[pltpu.VMEM((tq, 1), F32), pltpu.VMEM((tq, 2 * HEAD_DIM), F32)],
        compiler_params=_params("arbitrary", "arbitrary", "arbitrary"),
        name="attn",
    )(qkv, qkv, qkv, col, row, gn)


def _unit_lower_inverses(lowers):
    c = lowers[0].shape[0]
    r = lax.broadcasted_iota(jnp.int32, (c, c), 0)
    cc = lax.broadcasted_iota(jnp.int32, (c, c), 1)
    eye = jnp.where(r == cc, 1.0, 0.0)
    pair = (r ^ cc) == 1
    invs = [eye - jnp.where(pair, low, 0.0) for low in lowers]
    level = 1
    while (1 << level) < c:
        sel = ((r >> level) ^ (cc >> level)) == 1
        inv_b = [inv.astype(BF16) for inv in invs]
        half = [_dot(ib, jnp.where(sel, low, 0.0).astype(BF16)).astype(BF16) for ib, low in zip(inv_b, lowers)]
        invs = [inv - _dot(hf, ib) for inv, hf, ib in zip(invs, half, inv_b)]
        level += 1
    return invs


def _gdn_kernel(t_tokens, q_ref, k_ref, v_ref, z_ref, col_ref, row_ref, gn_ref, o_ref, state_scr):
    t = pl.program_id(1)
    c = GDN_CHUNK
    n_chunks = t_tokens // c

    @pl.when(t == 0)
    def _():
        state_scr[...] = jnp.zeros_like(state_scr)

    col = col_ref[0]
    r = lax.broadcasted_iota(jnp.int32, (c, c), 0)
    cc = lax.broadcasted_iota(jnp.int32, (c, c), 1)
    incl = cc <= r
    strict = cc < r
    chains = [(hh, ci) for ci in range(n_chunks) for hh in range(N_HEADS)]
    g_all = [_pick_lane(col, N_HEADS + hh) for hh in range(N_HEADS)]
    b_all = [_pick_lane(col, 2 * N_HEADS + hh) for hh in range(N_HEADS)]

    def tile(ref, hh, ci):
        return ref[0, ci * c:(ci + 1) * c, hh * HEAD_DIM:(hh + 1) * HEAD_DIM]

    lowers, a_intra = [], []
    for hh, ci in chains:
        k = tile(k_ref, hh, ci)
        g = g_all[hh][ci * c:(ci + 1) * c]
        g_row = row_ref[0, hh, :, ci * c:(ci + 1) * c]
        decay = jnp.exp(jnp.where(incl, g - g_row, -jnp.inf))
        k_b = k.astype(BF16)
        kb_b = (k * b_all[hh][ci * c:(ci + 1) * c]).astype(BF16)
        lowers.append(jnp.where(strict, _dot_nt(kb_b, k_b) * decay, 0.0))
        a_intra.append(jnp.where(incl, _dot_nt(tile(q_ref, hh, ci).astype(BF16), k_b) * decay, 0.0).astype(BF16))
    t_invs = [inv.astype(BF16) for inv in _unit_lower_inverses(lowers)]
    us, ws = [], []
    for (hh, ci), t_inv in zip(chains, t_invs):
        g = g_all[hh][ci * c:(ci + 1) * c]
        beta = b_all[hh][ci * c:(ci + 1) * c]
        us.append(_dot(t_inv, (tile(v_ref, hh, ci) * beta).astype(BF16)))
        ws.append(_dot(t_inv, (tile(k_ref, hh, ci) * (beta * jnp.exp(g))).astype(BF16)).astype(BF16))

    states = [state_scr[hh] for hh in range(N_HEADS)]
    for ci in range(n_chunks):
        idxs = [chains.index((hh, ci)) for hh in range(N_HEADS)]
        gs = [g_all[hh][ci * c:(ci + 1) * c] for hh in range(N_HEADS)]
        state_bs = [st.astype(BF16) for st in states]
        w_s = [_dot(ws[i], sb) for i, sb in zip(idxs, state_bs)]
        q_s = [_dot((tile(q_ref, hh, ci) * jnp.exp(gs[hh])).astype(BF16), state_bs[hh]) for hh in range(N_HEADS)]
        v_new_bs = [(us[i] - w_s[hh]).astype(BF16) for hh, i in enumerate(idxs)]
        for hh, i in enumerate(idxs):
            g = gs[hh]
            g_last = g[c - 1:c, :]
            o = q_s[hh] + _dot(a_intra[i], v_new_bs[hh])
            k_dec = (tile(k_ref, hh, ci) * jnp.exp(g_last - g)).astype(BF16)
            states[hh] = states[hh] * jnp.exp(g_last) + _dot_tn(k_dec, v_new_bs[hh])
            o_ref[0, ci * c:(ci + 1) * c, hh * HEAD_DIM:(hh + 1) * HEAD_DIM] = (
                _rms(o) * gn_ref[...] * tile(z_ref, hh, ci)).astype(o_ref.dtype)
    for hh in range(N_HEADS):
        state_scr[hh] = states[hh]


def _gdn(qkv, z, col, row, gn, t_tokens):
    b, s, _ = qkv.shape
    group = lambda g: pl.BlockSpec((1, t_tokens, GROUP_W), lambda bb, i: (bb, i, g))
    return pl.pallas_call(
        functools.partial(_gdn_kernel, t_tokens),
        grid=(b, s // t_tokens),
        in_specs=[group(0), group(1), group(2), group(0),
                  pl.BlockSpec((1, t_tokens, LANES), lambda bb, i: (bb, i, 0)),
                  pl.BlockSpec((1, N_HEADS, 1, t_tokens), lambda bb, i: (bb, 1, 0, i)),
                  pl.BlockSpec((1, HEAD_DIM), lambda bb, i: (0, 0))],
        out_specs=pl.BlockSpec((1, t_tokens, GROUP_W), lambda bb, i: (bb, i, 0)),
        out_shape=jax.ShapeDtypeStruct((b, s, GROUP_W), BF16),
        scratch_shapes=[pltpu.VMEM((N_HEADS, HEAD_DIM, HEAD_DIM), F32)],
        compiler_params=_params("arbitrary", "arbitrary"),
        name="gdn",
    )(qkv, qkv, qkv, z, col, row, gn)


def _oproj_kernel(of_ref, og_ref, wf_ref, wg_ref, x_ref, gt_ref, o_ref):
    y = _dot(of_ref[...], wf_ref[...]) + _dot(og_ref[...], wg_ref[...])
    o_ref[...] = x_ref[...] + gt_ref[0] * y


def _oproj(o_f, o_g, w_f, w_g, x, mod, tiles_per_batch, tm):
    n, d = x.shape
    row = lambda i: (i, 0)
    return pl.pallas_call(
        _oproj_kernel,
        grid=(n // tm,),
        in_specs=[pl.BlockSpec((tm, GROUP_W), row), pl.BlockSpec((tm, GROUP_W), row),
                  pl.BlockSpec((GROUP_W, d), lambda i: (0, 0)), pl.BlockSpec((GROUP_W, d), lambda i: (0, 0)),
                  pl.BlockSpec((tm, d), row),
                  pl.BlockSpec((1, 1, d), lambda i: ((i // tiles_per_batch) * N_MOD + 5, 0, 0))],
        out_specs=pl.BlockSpec((tm, d), row),
        out_shape=jax.ShapeDtypeStruct((n, d), F32),
        compiler_params=_params("arbitrary"),
        name="oproj",
    )(o_f, o_g, w_f, w_g, x, mod)


def kernel(x, c, ada_w, ada_b, norm_g, ffn_w_gate, ffn_w_up, ffn_w_down, w_in, w_out,
           fox_f_bias, fox_out_norm, gdn_conv, gdn_A_log, gdn_dt_bias, gdn_out_norm, final_norm):
    bsz, s, d = x.shape
    n = bsz * s
    tm = 512
    tiles_per_batch = s // tm
    l = 0
    bf = lambda t: t.astype(BF16)

    c_pad = jnp.zeros((SUBLANES, d), F32).at[:bsz].set(c)
    mod = _ada(c_pad, ada_w, ada_b, l)[:bsz].reshape(bsz * N_MOD, 1, d)

    xf = x.reshape(n, d)
    x1, h2 = _ffn(xf, mod, tiles_per_batch, tm, norm_g[l, 0][None], 0,
                  bf(ffn_w_gate[l, 0]), bf(ffn_w_up[l, 0]), bf(ffn_w_down[l, 0]),
                  norm_g[l, 1][None], 3, final=False)

    w = w_in[l]
    qkv_f = _proj(h2, bf(w[:, _OFF_QF:_OFF_FF]), 1024, False, BF16, "proj_f").reshape(bsz, s, 3 * GROUP_W)
    z_g = _proj(h2, bf(w[:, _OFF_ZG:]), 1024, True, F32, "proj_z").reshape(bsz, s, GROUP_W)
    h3 = h2.reshape(bsz, s, d)
    cw = jnp.zeros((SUBLANES, 3 * GROUP_W), F32).at[:CONV_W].set(gdn_conv[l])
    qkv_g = _proj_g(h3, bf(w[:, _OFF_QKVG:_OFF_AG]), cw, tm)

    pad = LANES - 3 * N_HEADS
    w_s = bf(jnp.concatenate([w[:, _OFF_FF:_OFF_QKVG], w[:, _OFF_AG:_OFF_ZG], jnp.zeros((d, pad), F32)], axis=1))
    zeros8 = jnp.zeros((N_HEADS,), F32)
    bias = jnp.concatenate([fox_f_bias[l], gdn_dt_bias[l], zeros8, jnp.zeros((pad,), F32)])[None]
    alog = jnp.concatenate([zeros8, gdn_A_log[l], zeros8, jnp.zeros((pad,), F32)])[None]
    col, row = _gates(h3, w_s, bias, alog, tm)
    row = row.reshape(bsz, row.shape[1], 1, s)

    o_f = _attn(qkv_f, col, row, fox_out_norm[l][None], min(2048, s), 256, 1024)
    o_g = _gdn(qkv_g, z_g, col, row, gdn_out_norm[l][None], 256)

    wo = bf(w_out[l])
    x2 = _oproj(o_f.reshape(n, GROUP_W), o_g.reshape(n, GROUP_W), wo[:GROUP_W], wo[GROUP_W:],
                x1, mod, tiles_per_batch, tm)

    out = _ffn(x2, mod, tiles_per_batch, tm, norm_g[l, 2][None], 6,
               bf(ffn_w_gate[l, 1]), bf(ffn_w_up[l, 1]), bf(ffn_w_down[l, 1]),
               final_norm[None], 0, final=True)
    return out.reshape(bsz, s, d)
```

```python
import functools
import math

import jax
import jax.numpy as jnp
from jax import lax
from jax.experimental import pallas as pl
from jax.experimental.pallas import tpu as pltpu

F32 = jnp.float32
BF16 = jnp.bfloat16

D_MODEL = 2048
HEAD_DIM = 128
N_HEADS = 8
GROUP_W = N_HEADS * HEAD_DIM
D_FF = 5632
CONV_W = 4
N_MOD = 9
EPS = 1e-6
LANES = 128
SUBLANES = 8
GDN_CHUNK = 128
LOG2E = math.log2(math.e)

_OFF_QF = 0
_OFF_FF = 3 * GROUP_W
_OFF_QKVG = _OFF_FF + N_HEADS
_OFF_AG = _OFF_QKVG + 3 * GROUP_W
_OFF_BG = _OFF_AG + N_HEADS
_OFF_ZG = _OFF_BG + N_HEADS

VMEM_LIMIT = 56 * 1024 * 1024


def _params(*sem):
    return pltpu.CompilerParams(dimension_semantics=sem, vmem_limit_bytes=VMEM_LIMIT)


def _sigmoid(x):
    return 1.0 / (1.0 + jnp.exp(-x))


def _silu(x):
    return x * _sigmoid(x)


def _softplus(x):
    return jnp.maximum(x, 0.0) + jnp.log1p(jnp.exp(-jnp.abs(x)))


def _dot(a, b):
    return jnp.dot(a, b, preferred_element_type=F32)


def _dot_nt(a, b):
    return lax.dot_general(a, b, (((1,), (1,)), ((), ())), preferred_element_type=F32)


def _dot_tn(a, b):
    return lax.dot_general(a, b, (((0,), (0,)), ((), ())), preferred_element_type=F32)


def _rms(x):
    return x * lax.rsqrt(jnp.mean(x * x, axis=-1, keepdims=True) + EPS)


def _ada_kernel(c_ref, w_ref, b_ref, o_ref):
    cond = _silu(c_ref[...])
    o_ref[...] = _dot(cond.astype(BF16), w_ref[...].astype(BF16)) + b_ref[...]


def _ada(c_pad, w, b, layer):
    rows, d = c_pad.shape
    n = w.shape[2]
    tn = 1024
    return pl.pallas_call(
        _ada_kernel,
        grid=(n // tn,),
        in_specs=[pl.BlockSpec((rows, d), lambda j: (0, 0)),
                  pl.BlockSpec((None, d, tn), lambda j: (layer, 0, j)),
                  pl.BlockSpec((1, tn), lambda j: (layer, j))],
        out_specs=pl.BlockSpec((rows, tn), lambda j: (0, j)),
        out_shape=jax.ShapeDtypeStruct((rows, n), F32),
        compiler_params=_params("arbitrary"),
        name="ada",
    )(c_pad, w, b)


_ROW_CHUNK = 64


def _row_loop(n_rows, body):
    for r in range(n_rows // _ROW_CHUNK):
        body(slice(r * _ROW_CHUNK, (r + 1) * _ROW_CHUNK))


def _inv_rms(x):
    return lax.rsqrt(jnp.mean(x * x, axis=-1, keepdims=True) + EPS)


def _ffn_kernel(final, x_ref, vec_ref, wgu_ref, wd_ref, *rest):
    if final:
        o_ref, h_scr, acc_scr, r_scr = rest
    else:
        o_ref, h2_ref, h_scr, acc_scr, r_scr = rest
    f = pl.program_id(1)
    tm = x_ref.shape[0]
    tf = wd_ref.shape[0]
    vec = lambda r: vec_ref[0, r:r + 1, :]

    @pl.when(f == 0)
    def _():
        gain = vec(0) * (1.0 + vec(2))
        shift = vec(1)

        def stats(rows):
            r_scr[rows, :] = _inv_rms(x_ref[rows, :])
            acc_scr[rows, :] = jnp.zeros((_ROW_CHUNK, acc_scr.shape[1]), F32)

        def modulate(rows):
            h_scr[rows, :] = (x_ref[rows, :] * r_scr[rows, :] * gain + shift).astype(BF16)

        _row_loop(tm, stats)
        _row_loop(tm, modulate)

    h = h_scr[...]
    act = _silu(_dot(h, wgu_ref[:, :tf])) * _dot(h, wgu_ref[:, tf:])
    acc_scr[...] += _dot(act.astype(BF16), wd_ref[...])

    @pl.when(f == pl.num_programs(1) - 1)
    def _():
        half_gate = 0.5 * vec(3)
        gain2 = vec(4) if final else vec(4) * (1.0 + vec(6))

        def residual(rows):
            xn = x_ref[rows, :] + half_gate * acc_scr[rows, :]
            o_ref[rows, :] = xn
            r_scr[rows, :] = _inv_rms(xn)

        def normalise(rows):
            y = o_ref[rows, :] * r_scr[rows, :] * gain2
            if final:
                o_ref[rows, :] = y
            else:
                h2_ref[rows, :] = (y + vec(5)).astype(BF16)

        _row_loop(tm, residual)
        _row_loop(tm, normalise)


_FFN_TF = 512


def _ffn_weights(wg, wu, wd):
    d, dff = wg.shape
    nf = dff // _FFN_TF
    wgu = jnp.concatenate([wg.reshape(d, nf, _FFN_TF), wu.reshape(d, nf, _FFN_TF)], axis=2)
    return wgu.reshape(d, 2 * dff).astype(BF16), wd.astype(BF16)


def _ffn(x, vecs, tiles_per_batch, tm, wgu, wd, final):
    n, d = x.shape
    dff = wd.shape[0]
    tf = _FFN_TF
    row = lambda i, f: (i, 0)
    in_specs = [pl.BlockSpec((tm, d), row),
                pl.BlockSpec((1, SUBLANES, d), lambda i, f: (i // tiles_per_batch, 0, 0)),
                pl.BlockSpec((d, 2 * tf), lambda i, f: (0, f)),
                pl.BlockSpec((tf, d), lambda i, f: (f, 0))]
    if final:
        out_shape = jax.ShapeDtypeStruct((n, d), F32)
        out_specs = pl.BlockSpec((tm, d), row)
    else:
        out_shape = (jax.ShapeDtypeStruct((n, d), F32), jax.ShapeDtypeStruct((n, d), BF16))
        out_specs = (pl.BlockSpec((tm, d), row), pl.BlockSpec((tm, d), row))
    return pl.pallas_call(
        functools.partial(_ffn_kernel, final),
        grid=(n // tm, dff // tf),
        in_specs=in_specs,
        out_specs=out_specs,
        out_shape=out_shape,
        scratch_shapes=[pltpu.VMEM((tm, d), BF16), pltpu.VMEM((tm, d), F32), pltpu.VMEM((tm, 1), F32)],
        compiler_params=_params("arbitrary", "arbitrary"),
        name="ffn_final" if final else "ffn",
    )(x, vecs, wgu, wd)


def _proj_kernel(gated, h_ref, w_ref, o_ref):
    y = _dot(h_ref[...], w_ref[...])
    if gated:
        y = _silu(y)
    else:
        y = y * jnp.where(pl.program_id(0) == 0, LOG2E / math.sqrt(HEAD_DIM), 1.0)
    o_ref[...] = y.astype(o_ref.dtype)


def _proj(h, w, c, tm, gated, out_dtype, name):
    n, d = h.shape
    tn = GROUP_W
    return pl.pallas_call(
        functools.partial(_proj_kernel, gated),
        grid=(c // tn, n // tm),
        in_specs=[pl.BlockSpec((tm, d), lambda j, i: (i, 0)),
                  pl.BlockSpec((d, tn), lambda j, i: (0, j))],
        out_specs=pl.BlockSpec((tm, tn), lambda j, i: (i, j)),
        out_shape=jax.ShapeDtypeStruct((n, c), out_dtype),
        compiler_params=_params("arbitrary", "arbitrary"),
        name=name,
    )(h, w)


_CONV_COLS = 2 * HEAD_DIM


def _proj_g_kernel(h_ref, w_ref, cw_ref, o_ref, prev_scr, *bufs):
    j = pl.program_id(0)
    s = pl.program_id(2)
    tm = h_ref.shape[0]
    h = h_ref[...]
    cw = cw_ref[...]
    prev = jnp.where(s == 0, 0.0, prev_scr[...])
    gain = jnp.where(j == 0, HEAD_DIM ** -0.5, 1.0)
    for p, buf in enumerate(bufs):
        cols = slice(p * _CONV_COLS, (p + 1) * _CONV_COLS)
        y = _dot(h, w_ref[:, cols])
        buf[0:SUBLANES, :] = prev[:, cols]
        buf[SUBLANES:, :] = y
        prev_scr[:, cols] = y[tm - SUBLANES:, :]
        out = y * cw[CONV_W - 1:CONV_W, cols]
        for sh in range(1, CONV_W):
            out = out + buf[SUBLANES - sh:SUBLANES - sh + tm, :] * cw[CONV_W - 1 - sh:CONV_W - sh, cols]
        out = _silu(out)
        for hh in range(_CONV_COLS // HEAD_DIM):
            t = out[:, hh * HEAD_DIM:(hh + 1) * HEAD_DIM]
            nrm = lax.rsqrt(jnp.sum(t * t, axis=-1, keepdims=True) + EPS) * gain
            c0 = p * _CONV_COLS + hh * HEAD_DIM
            o_ref[0, :, c0:c0 + HEAD_DIM] = t * jnp.where(j < 2, nrm, 1.0)


def _proj_g(h3, w, cw, tm):
    b, s, d = h3.shape
    c = w.shape[1]
    tn = GROUP_W
    n_pieces = tn // _CONV_COLS
    return pl.pallas_call(
        _proj_g_kernel,
        grid=(c // tn, b, s // tm),
        in_specs=[pl.BlockSpec((None, tm, d), lambda j, bb, i: (bb, i, 0)),
                  pl.BlockSpec((d, tn), lambda j, bb, i: (0, j)),
                  pl.BlockSpec((SUBLANES, tn), lambda j, bb, i: (0, j))],
        out_specs=pl.BlockSpec((1, tm, tn), lambda j, bb, i: (bb, i, j)),
        out_shape=jax.ShapeDtypeStruct((b, s, c), F32),
        scratch_shapes=[pltpu.VMEM((SUBLANES, tn), F32)]
        + [pltpu.VMEM((tm + SUBLANES, _CONV_COLS), F32) for _ in range(n_pieces)],
        compiler_params=_params("arbitrary", "arbitrary", "arbitrary"),
        name="proj_g",
    )(h3, w, cw)


def _gates_kernel(h_ref, w_ref, bias_ref, alog_ref, col_ref, row_ref, carry_scr):
    s = pl.program_id(1)
    y = _dot(h_ref[...], w_ref[...])
    tm = y.shape[0]
    lane = lax.broadcasted_iota(jnp.int32, (1, LANES), 1)
    is_f = lane < N_HEADS
    is_a = (lane >= N_HEADS) & (lane < 2 * N_HEADS)
    yb = y + bias_ref[...]
    sp = _softplus(jnp.where(is_f, -yb, yb))
    val = jnp.where(is_f, -sp, jnp.where(is_a, -jnp.exp(alog_ref[...]) * sp, _sigmoid(y)))

    @pl.when(s == 0)
    def _():
        carry_scr[...] = jnp.zeros_like(carry_scr)

    rowi = lax.broadcasted_iota(jnp.int32, (tm, LANES), 0)
    rowc = rowi % GDN_CHUNK
    cum = val
    cumc = val
    sh = 1
    while sh < tm:
        cum = cum + jnp.where(rowi >= sh, pltpu.roll(cum, sh, 0), 0.0)
        if sh < GDN_CHUNK:
            cumc = cumc + jnp.where(rowc >= sh, pltpu.roll(cumc, sh, 0), 0.0)
        sh *= 2
    cum = cum + carry_scr[0:1, :]
    carry_scr[...] = jnp.broadcast_to(cum[tm - 1:tm, :], carry_scr.shape)
    out = jnp.where(is_f, cum * LOG2E, jnp.where(is_a, cumc, val))
    col_ref[0] = out
    row_ref[0] = out.T[:row_ref.shape[1], :]


def _gates(h3, w, bias, alog, tm):
    b, s, d = h3.shape
    nrow = 32
    return pl.pallas_call(
        _gates_kernel,
        grid=(b, s // tm),
        in_specs=[pl.BlockSpec((None, tm, d), lambda bb, i: (bb, i, 0)),
                  pl.BlockSpec((d, LANES), lambda bb, i: (0, 0)),
                  pl.BlockSpec((1, LANES), lambda bb, i: (0, 0)),
                  pl.BlockSpec((1, LANES), lambda bb, i: (0, 0))],
        out_specs=(pl.BlockSpec((1, tm, LANES), lambda bb, i: (bb, i, 0)),
                   pl.BlockSpec((1, nrow, tm), lambda bb, i: (bb, 0, i))),
        out_shape=(jax.ShapeDtypeStruct((b, s, LANES), F32),
                   jax.ShapeDtypeStruct((b, nrow, s), F32)),
        scratch_shapes=[pltpu.VMEM((SUBLANES, LANES), F32)],
        compiler_params=_params("arbitrary", "arbitrary"),
        name="gates",
    )(h3, w, bias, alog)


def _pick_lane(x, idx):
    lane = lax.broadcasted_iota(jnp.int32, x.shape, 1)
    return jnp.sum(jnp.where(lane == idx, x, 0.0), axis=-1, keepdims=True)


def _attn_kernel(tq, rg, tk, q_ref, k_ref, v_ref, cq_ref, ck_ref, gn_ref, o_ref, m_scr, acc_scr):
    h = pl.program_id(1)
    qi = pl.program_id(2)
    ng = tq // rg
    ahead = 3
    cq_all = _pick_lane(cq_ref[0], h)
    qs = [q_ref[0, g * rg:(g + 1) * rg, :] for g in range(ng)]
    cqs = [cq_all[g * rg:(g + 1) * rg] for g in range(ng)]
    q0 = pl.multiple_of(qi * tq, tq)

    def step(off, widths, first):
        wmax = max(widths)
        k = k_ref[0, pl.ds(off, wmax), :]
        v1 = jnp.concatenate([v_ref[0, pl.ds(off, wmax), :], jnp.ones((wmax, HEAD_DIM), BF16)], axis=1)
        ck = ck_ref[0, 0, :, pl.ds(off, wmax)]
        ss = {g: _dot_nt(qs[g], k[:widths[g]]) for g in range(min(ahead, ng))}
        for g in range(ng):
            w = widths[g]
            rows = slice(g * rg, (g + 1) * rg)
            if g + ahead < ng:
                ss[g + ahead] = _dot_nt(qs[g + ahead], k[:widths[g + ahead]])
            s = ss.pop(g) - ck[:, :w]
            if first:
                r = lax.broadcasted_iota(jnp.int32, (rg, w), 0) + g * rg
                c = lax.broadcasted_iota(jnp.int32, (rg, w), 1)
                s = jnp.where(c <= r, s, -jnp.inf)
                m_new = jnp.max(s, axis=-1, keepdims=True) + cqs[g]
            else:
                m_old = m_scr[rows, :]
                m_new = jnp.maximum(m_old, jnp.max(s, axis=-1, keepdims=True) + cqs[g])
            p = jnp.exp2(s - (m_new - cqs[g]))
            pv = _dot(p.astype(BF16), v1[:w])
            acc_scr[rows, :] = pv if first else jnp.exp2(m_old - m_new) * acc_scr[rows, :] + pv
            m_scr[rows, :] = m_new

    step(q0, [(g + 1) * rg for g in range(ng)], True)

    def body(j, carry):
        step(pl.multiple_of(j * tk, tk), [tk] * ng, False)
        return carry

    lax.fori_loop(0, qi * (tq // tk), body, 0)
    for g in range(ng):
        rows = slice(g * rg, (g + 1) * rg)
        o = acc_scr[rows, :HEAD_DIM] / acc_scr[rows, HEAD_DIM:]
        o_ref[0, rows, :] = (_rms(o) * gn_ref[...]).astype(o_ref.dtype)


def _attn(qkv, col, row, gn, tq, rg, tk):
    b, s, _ = qkv.shape
    return pl.pallas_call(
        functools.partial(_attn_kernel, tq, rg, tk),
        grid=(b, N_HEADS, s // tq),
        in_specs=[pl.BlockSpec((1, tq, HEAD_DIM), lambda bb, h, i: (bb, i, h)),
                  pl.BlockSpec((1, s, HEAD_DIM), lambda bb, h, i: (bb, 0, N_HEADS + h)),
                  pl.BlockSpec((1, s, HEAD_DIM), lambda bb, h, i: (bb, 0, 2 * N_HEADS + h)),
                  pl.BlockSpec((1, tq, LANES), lambda bb, h, i: (bb, i, 0)),
                  pl.BlockSpec((1, 1, 1, s), lambda bb, h, i: (bb, h, 0, 0)),
                  pl.BlockSpec((1, HEAD_DIM), lambda bb, h, i: (0, 0))],
        out_specs=pl.BlockSpec((1, tq, HEAD_DIM), lambda bb, h, i: (bb, i, h)),
        out_shape=jax.ShapeDtypeStruct((b, s, GROUP_W), BF16),
        scratch_shapes=[pltpu.VMEM((tq, 1), F32), pltpu.VMEM((tq, 2 * HEAD_DIM), F32)],
        compiler_params=_params("arbitrary", "arbitrary", "arbitrary"),
        name="attn",
    )(qkv, qkv, qkv, col, row, gn)


def _unit_lower_inverses(lowers):
    c = lowers[0].shape[0]
    r = lax.broadcasted_iota(jnp.int32, (c, c), 0)
    cc = lax.broadcasted_iota(jnp.int32, (c, c), 1)
    eye = jnp.where(r == cc, 1.0, 0.0)
    pair = (r ^ cc) == 1
    invs = [eye - jnp.where(pair, low, 0.0) for low in lowers]
    level = 1
    while (1 << level) < c:
        sel = ((r >> level) ^ (cc >> level)) == 1
        inv_b = [inv.astype(BF16) for inv in invs]
        half = [_dot(ib, jnp.where(sel, low, 0.0).astype(BF16)).astype(BF16) for ib, low in zip(inv_b, lowers)]
        invs = [inv - _dot(hf, ib) for inv, hf, ib in zip(invs, half, inv_b)]
        level += 1
    return invs


def _gdn_kernel(t_tokens, q_ref, k_ref, v_ref, z_ref, col_ref, row_ref, gn_ref, o_ref, state_scr):
    t = pl.program_id(1)
    c = GDN_CHUNK
    n_chunks = t_tokens // c

    @pl.when(t == 0)
    def _():
        state_scr[...] = jnp.zeros_like(state_scr)

    col = col_ref[0]
    r = lax.broadcasted_iota(jnp.int32, (c, c), 0)
    cc = lax.broadcasted_iota(jnp.int32, (c, c), 1)
    incl = cc <= r
    strict = cc < r
    chains = [(hh, ci) for ci in range(n_chunks) for hh in range(N_HEADS)]
    g_all = [_pick_lane(col, N_HEADS + hh) for hh in range(N_HEADS)]
    b_all = [_pick_lane(col, 2 * N_HEADS + hh) for hh in range(N_HEADS)]

    def tile(ref, hh, ci):
        return ref[0, ci * c:(ci + 1) * c, hh * HEAD_DIM:(hh + 1) * HEAD_DIM]

    lowers, a_intra = [], []
    for hh, ci in chains:
        k = tile(k_ref, hh, ci)
        g = g_all[hh][ci * c:(ci + 1) * c]
        g_row = row_ref[0, hh, :, ci * c:(ci + 1) * c]
        decay = jnp.exp(jnp.where(incl, g - g_row, -jnp.inf))
        k_b = k.astype(BF16)
        kb_b = (k * b_all[hh][ci * c:(ci + 1) * c]).astype(BF16)
        lowers.append(jnp.where(strict, _dot_nt(kb_b, k_b) * decay, 0.0))
        a_intra.append(jnp.where(incl, _dot_nt(tile(q_ref, hh, ci).astype(BF16), k_b) * decay, 0.0).astype(BF16))
    t_invs = [inv.astype(BF16) for inv in _unit_lower_inverses(lowers)]
    us, ws = [], []
    for (hh, ci), t_inv in zip(chains, t_invs):
        g = g_all[hh][ci * c:(ci + 1) * c]
        beta = b_all[hh][ci * c:(ci + 1) * c]
        us.append(_dot(t_inv, (tile(v_ref, hh, ci) * beta).astype(BF16)))
        ws.append(_dot(t_inv, (tile(k_ref, hh, ci) * (beta * jnp.exp(g))).astype(BF16)).astype(BF16))

    states = [state_scr[hh] for hh in range(N_HEADS)]
    for ci in range(n_chunks):
        idxs = [chains.index((hh, ci)) for hh in range(N_HEADS)]
        gs = [g_all[hh][ci * c:(ci + 1) * c] for hh in range(N_HEADS)]
        state_bs = [st.astype(BF16) for st in states]
        w_s = [_dot(ws[i], sb) for i, sb in zip(idxs, state_bs)]
        q_s = [_dot((tile(q_ref, hh, ci) * jnp.exp(gs[hh])).astype(BF16), state_bs[hh]) for hh in range(N_HEADS)]
        v_new_bs = [(us[i] - w_s[hh]).astype(BF16) for hh, i in enumerate(idxs)]
        for hh, i in enumerate(idxs):
            g = gs[hh]
            g_last = g[c - 1:c, :]
            o = q_s[hh] + _dot(a_intra[i], v_new_bs[hh])
            k_dec = (tile(k_ref, hh, ci) * jnp.exp(g_last - g)).astype(BF16)
            states[hh] = states[hh] * jnp.exp(g_last) + _dot_tn(k_dec, v_new_bs[hh])
            o_ref[0, ci * c:(ci + 1) * c, hh * HEAD_DIM:(hh + 1) * HEAD_DIM] = (
                _rms(o) * gn_ref[...] * tile(z_ref, hh, ci)).astype(o_ref.dtype)
    for hh in range(N_HEADS):
        state_scr[hh] = states[hh]


def _gdn(qkv, z, col, row, gn, t_tokens):
    b, s, _ = qkv.shape
    group = lambda g: pl.BlockSpec((1, t_tokens, GROUP_W), lambda bb, i: (bb, i, g))
    return pl.pallas_call(
        functools.partial(_gdn_kernel, t_tokens),
        grid=(b, s // t_tokens),
        in_specs=[group(0), group(1), group(2), group(0),
                  pl.BlockSpec((1, t_tokens, LANES), lambda bb, i: (bb, i, 0)),
                  pl.BlockSpec((1, N_HEADS, 1, t_tokens), lambda bb, i: (bb, 1, 0, i)),
                  pl.BlockSpec((1, HEAD_DIM), lambda bb, i: (0, 0))],
        out_specs=pl.BlockSpec((1, t_tokens, GROUP_W), lambda bb, i: (bb, i, 0)),
        out_shape=jax.ShapeDtypeStruct((b, s, GROUP_W), BF16),
        scratch_shapes=[pltpu.VMEM((N_HEADS, HEAD_DIM, HEAD_DIM), F32)],
        compiler_params=_params("arbitrary", "arbitrary"),
        name="gdn",
    )(qkv, qkv, qkv, z, col, row, gn)


def _oproj_kernel(of_ref, og_ref, wf_ref, wg_ref, x_ref, gt_ref, o_ref):
    y = _dot(of_ref[...], wf_ref[...]) + _dot(og_ref[...], wg_ref[...])
    o_ref[...] = x_ref[...] + gt_ref[0] * y


def _oproj(o_f, o_g, w_f, w_g, x, mod, tiles_per_batch, tm):
    n, d = x.shape
    row = lambda i: (i, 0)
    return pl.pallas_call(
        _oproj_kernel,
        grid=(n // tm,),
        in_specs=[pl.BlockSpec((tm, GROUP_W), row), pl.BlockSpec((tm, GROUP_W), row),
                  pl.BlockSpec((GROUP_W, d), lambda i: (0, 0)), pl.BlockSpec((GROUP_W, d), lambda i: (0, 0)),
                  pl.BlockSpec((tm, d), row),
                  pl.BlockSpec((1, 1, d), lambda i: ((i // tiles_per_batch) * N_MOD + 5, 0, 0))],
        out_specs=pl.BlockSpec((tm, d), row),
        out_shape=jax.ShapeDtypeStruct((n, d), F32),
        compiler_params=_params("arbitrary"),
        name="oproj",
    )(o_f, o_g, w_f, w_g, x, mod)


def kernel(x, c, ada_w, ada_b, norm_g, ffn_w_gate, ffn_w_up, ffn_w_down, w_in, w_out,
           fox_f_bias, fox_out_norm, gdn_conv, gdn_A_log, gdn_dt_bias, gdn_out_norm, final_norm):
    bsz, s, d = x.shape
    n = bsz * s
    tm = 512
    tiles_per_batch = s // tm
    l = 0
    bf = lambda t: t.astype(BF16)

    c_pad = jnp.zeros((SUBLANES, d), F32).at[:bsz].set(c)
    mod9 = _ada(c_pad, ada_w, ada_b, l)[:bsz].reshape(bsz, N_MOD, d)
    mod = mod9.reshape(bsz * N_MOD, 1, d)
    rep = lambda v: jnp.broadcast_to(v[None, None, :], (bsz, 1, d))
    zero = jnp.zeros((bsz, 1, d), F32)
    vecs1 = jnp.concatenate([rep(norm_g[l, 0]), mod9[:, 0:3], rep(norm_g[l, 1]), mod9[:, 3:5], zero], axis=1)
    vecs2 = jnp.concatenate([rep(norm_g[l, 2]), mod9[:, 6:9], rep(final_norm), zero, zero, zero], axis=1)

    xf = x.reshape(n, d)
    x1, h2 = _ffn(xf, vecs1, tiles_per_batch, tm,
                  *_ffn_weights(ffn_w_gate[l, 0], ffn_w_up[l, 0], ffn_w_down[l, 0]), final=False)

    w = bf(w_in[l])
    qkv_f = _proj(h2, w, 3 * GROUP_W, 1024, False, BF16, "proj_f").reshape(bsz, s, 3 * GROUP_W)
    z_g = _proj(h2, w[:, _OFF_ZG:], GROUP_W, 1024, True, F32, "proj_z").reshape(bsz, s, GROUP_W)
    h3 = h2.reshape(bsz, s, d)
    cw = jnp.zeros((SUBLANES, 3 * GROUP_W), F32).at[:CONV_W].set(gdn_conv[l])
    qkv_g = _proj_g(h3, w[:, _OFF_QKVG:_OFF_AG], cw, min(1024, s))

    pad = LANES - 3 * N_HEADS
    w_s = jnp.concatenate([w[:, _OFF_FF:_OFF_QKVG], w[:, _OFF_AG:_OFF_ZG], jnp.zeros((d, pad), BF16)], axis=1)
    zeros8 = jnp.zeros((N_HEADS,), F32)
    bias = jnp.concatenate([fox_f_bias[l], gdn_dt_bias[l], zeros8, jnp.zeros((pad,), F32)])[None]
    alog = jnp.concatenate([zeros8, gdn_A_log[l], zeros8, jnp.zeros((pad,), F32)])[None]
    col, row = _gates(h3, w_s, bias, alog, tm)
    row = row.reshape(bsz, row.shape[1], 1, s)

    o_f = _attn(qkv_f, col, row, fox_out_norm[l][None], min(4096, s), 256, 1024)
    o_g = _gdn(qkv_g, z_g, col, row, gdn_out_norm[l][None], 256)

    wo = bf(w_out[l])
    x2 = _oproj(o_f.reshape(n, GROUP_W), o_g.reshape(n, GROUP_W), wo[:GROUP_W], wo[GROUP_W:],
                x1, mod, tiles_per_batch, tm)

    out = _ffn(x2, vecs2, tiles_per_batch, tm,
               *_ffn_weights(ffn_w_gate[l, 1], ffn_w_up[l, 1], ffn_w_down[l, 1]), final=True)
    return out.reshape(bsz, s, d)
```

```python
import functools
import math

import jax
import jax.numpy as jnp
from jax import lax
from jax.experimental import pallas as pl
from jax.experimental.pallas import tpu as pltpu

F32 = jnp.float32
BF16 = jnp.bfloat16

D_MODEL = 2048
HEAD_DIM = 128
N_HEADS = 8
GROUP_W = N_HEADS * HEAD_DIM
D_FF = 5632
CONV_W = 4
N_MOD = 9
EPS = 1e-6
LANES = 128
SUBLANES = 8
GDN_CHUNK = 128
LOG2E = math.log2(math.e)

_OFF_QF = 0
_OFF_FF = 3 * GROUP_W
_OFF_QKVG = _OFF_FF + N_HEADS
_OFF_AG = _OFF_QKVG + 3 * GROUP_W
_OFF_BG = _OFF_AG + N_HEADS
_OFF_ZG = _OFF_BG + N_HEADS

VMEM_LIMIT = 56 * 1024 * 1024


def _params(*sem):
    return pltpu.CompilerParams(dimension_semantics=sem, vmem_limit_bytes=VMEM_LIMIT)


def _sigmoid(x):
    return 1.0 / (1.0 + jnp.exp(-x))


def _silu(x):
    return x * _sigmoid(x)


def _softplus(x):
    return jnp.maximum(x, 0.0) + jnp.log1p(jnp.exp(-jnp.abs(x)))


def _dot(a, b):
    return jnp.dot(a, b, preferred_element_type=F32)


def _dot_nt(a, b):
    return lax.dot_general(a, b, (((1,), (1,)), ((), ())), preferred_element_type=F32)


def _dot_tn(a, b):
    return lax.dot_general(a, b, (((0,), (0,)), ((), ())), preferred_element_type=F32)


def _rms(x):
    return x * lax.rsqrt(jnp.mean(x * x, axis=-1, keepdims=True) + EPS)


def _ada_kernel(c_ref, w_ref, b_ref, o_ref):
    cond = _silu(c_ref[...])
    o_ref[...] = _dot(cond.astype(BF16), w_ref[...].astype(BF16)) + b_ref[...]


def _ada(c_pad, w, b, layer):
    rows, d = c_pad.shape
    n = w.shape[2]
    tn = 1024
    return pl.pallas_call(
        _ada_kernel,
        grid=(n // tn,),
        in_specs=[pl.BlockSpec((rows, d), lambda j: (0, 0)),
                  pl.BlockSpec((None, d, tn), lambda j: (layer, 0, j)),
                  pl.BlockSpec((1, tn), lambda j: (layer, j))],
        out_specs=pl.BlockSpec((rows, tn), lambda j: (0, j)),
        out_shape=jax.ShapeDtypeStruct((rows, n), F32),
        compiler_params=_params("arbitrary"),
        name="ada",
    )(c_pad, w, b)


_ROW_CHUNK = 64


def _inv_rms(x):
    return lax.rsqrt(jnp.mean(x * x, axis=-1, keepdims=True) + EPS)


def _ffn_kernel(final, x_ref, vec_ref, wg_ref, wu_ref, wd_ref, *rest):
    if final:
        o_ref, h_scr, acc_scr, r_scr = rest
    else:
        o_ref, h2_ref, h_scr, acc_scr, r_scr = rest
    f = pl.program_id(1)
    last = pl.num_programs(1) - 1
    tm = x_ref.shape[0]
    vec = lambda r: vec_ref[0, r:r + 1, :]
    halves = [slice(0, tm // 2), slice(tm // 2, tm)]

    def chunks(rows):
        return [slice(r, r + _ROW_CHUNK) for r in range(rows.start, rows.stop, _ROW_CHUNK)]

    def swiglu_down(rows):
        h = h_scr[rows, :]
        act = _silu(_dot(h, wg_ref[...])) * _dot(h, wu_ref[...])
        return _dot(act.astype(BF16), wd_ref[...])

    @pl.when(f == 0)
    def _():
        gain = vec(0) * (1.0 + vec(2))
        shift = vec(1)
        for rows in halves:
            for c in chunks(rows):
                r_scr[c, :] = _inv_rms(x_ref[c, :])
            for c in chunks(rows):
                h_scr[c, :] = (x_ref[c, :] * r_scr[c, :] * gain + shift).astype(BF16)
            acc_scr[rows, :] = swiglu_down(rows)

    @pl.when((f > 0) & (f < last))
    def _():
        acc_scr[...] += swiglu_down(slice(0, tm))

    @pl.when(f == last)
    def _():
        half_gate = 0.5 * vec(3)
        gain2 = vec(4) if final else vec(4) * (1.0 + vec(6))
        for rows in halves:
            acc_scr[rows, :] += swiglu_down(rows)
            for c in chunks(rows):
                xn = x_ref[c, :] + half_gate * acc_scr[c, :]
                o_ref[c, :] = xn
                r_scr[c, :] = _inv_rms(xn)
            for c in chunks(rows):
                y = o_ref[c, :] * r_scr[c, :] * gain2
                if final:
                    o_ref[c, :] = y
                else:
                    h2_ref[c, :] = (y + vec(5)).astype(BF16)


_FFN_TF = 512


def _ffn(x, vecs, tiles_per_batch, tm, wg, wu, wd, which, final):
    n, d = x.shape
    dff = wd.shape[1]
    tf = _FFN_TF
    row = lambda i, f: (i, 0)
    in_specs = [pl.BlockSpec((tm, d), row),
                pl.BlockSpec((1, SUBLANES, d), lambda i, f: (i // tiles_per_batch, 0, 0)),
                pl.BlockSpec((None, d, tf), lambda i, f: (which, 0, f)),
                pl.BlockSpec((None, d, tf), lambda i, f: (which, 0, f)),
                pl.BlockSpec((None, tf, d), lambda i, f: (which, f, 0))]
    if final:
        out_shape = jax.ShapeDtypeStruct((n, d), F32)
        out_specs = pl.BlockSpec((tm, d), row)
    else:
        out_shape = (jax.ShapeDtypeStruct((n, d), F32), jax.ShapeDtypeStruct((n, d), BF16))
        out_specs = (pl.BlockSpec((tm, d), row), pl.BlockSpec((tm, d), row))
    return pl.pallas_call(
        functools.partial(_ffn_kernel, final),
        grid=(n // tm, dff // tf),
        in_specs=in_specs,
        out_specs=out_specs,
        out_shape=out_shape,
        scratch_shapes=[pltpu.VMEM((tm, d), BF16), pltpu.VMEM((tm, d), F32), pltpu.VMEM((tm, 1), F32)],
        compiler_params=_params("arbitrary", "arbitrary"),
        name="ffn_final" if final else "ffn",
    )(x, vecs, wg, wu, wd)


def _proj_kernel(gated, h_ref, w_ref, o_ref):
    y = _dot(h_ref[...], w_ref[...])
    if gated:
        y = _silu(y)
    else:
        y = y * jnp.where(pl.program_id(0) == 0, LOG2E / math.sqrt(HEAD_DIM), 1.0)
    o_ref[...] = y.astype(o_ref.dtype)


def _proj(h, w, c, tm, gated, out_dtype, name):
    n, d = h.shape
    tn = GROUP_W
    return pl.pallas_call(
        functools.partial(_proj_kernel, gated),
        grid=(c // tn, n // tm),
        in_specs=[pl.BlockSpec((tm, d), lambda j, i: (i, 0)),
                  pl.BlockSpec((d, tn), lambda j, i: (0, j))],
        out_specs=pl.BlockSpec((tm, tn), lambda j, i: (i, j)),
        out_shape=jax.ShapeDtypeStruct((n, c), out_dtype),
        compiler_params=_params("arbitrary", "arbitrary"),
        name=name,
    )(h, w)


_CONV_COLS = 2 * HEAD_DIM


def _proj_g_kernel(h_ref, w_ref, cw_ref, o_ref, prev_scr, *bufs):
    j = pl.program_id(0)
    s = pl.program_id(2)
    tm = h_ref.shape[0]
    h = h_ref[...]
    cw = cw_ref[...]
    prev = jnp.where(s == 0, 0.0, prev_scr[...])
    gain = jnp.where(j == 0, HEAD_DIM ** -0.5, 1.0)
    for p, buf in enumerate(bufs):
        cols = slice(p * _CONV_COLS, (p + 1) * _CONV_COLS)
        y = _dot(h, w_ref[:, cols])
        buf[0:SUBLANES, :] = prev[:, cols]
        buf[SUBLANES:, :] = y
        prev_scr[:, cols] = y[tm - SUBLANES:, :]
        out = y * cw[CONV_W - 1:CONV_W, cols]
        for sh in range(1, CONV_W):
            out = out + buf[SUBLANES - sh:SUBLANES - sh + tm, :] * cw[CONV_W - 1 - sh:CONV_W - sh, cols]
        out = _silu(out)
        for hh in range(_CONV_COLS // HEAD_DIM):
            t = out[:, hh * HEAD_DIM:(hh + 1) * HEAD_DIM]
            nrm = lax.rsqrt(jnp.sum(t * t, axis=-1, keepdims=True) + EPS) * gain
            c0 = p * _CONV_COLS + hh * HEAD_DIM
            o_ref[0, :, c0:c0 + HEAD_DIM] = t * jnp.where(j < 2, nrm, 1.0)


def _proj_g(h3, w, cw, tm):
    b, s, d = h3.shape
    c = w.shape[1]
    tn = GROUP_W
    n_pieces = tn // _CONV_COLS
    return pl.pallas_call(
        _proj_g_kernel,
        grid=(c // tn, b, s // tm),
        in_specs=[pl.BlockSpec((None, tm, d), lambda j, bb, i: (bb, i, 0)),
                  pl.BlockSpec((d, tn), lambda j, bb, i: (0, j)),
                  pl.BlockSpec((SUBLANES, tn), lambda j, bb, i: (0, j))],
        out_specs=pl.BlockSpec((1, tm, tn), lambda j, bb, i: (bb, i, j)),
        out_shape=jax.ShapeDtypeStruct((b, s, c), F32),
        scratch_shapes=[pltpu.VMEM((SUBLANES, tn), F32)]
        + [pltpu.VMEM((tm + SUBLANES, _CONV_COLS), F32) for _ in range(n_pieces)],
        compiler_params=_params("arbitrary", "arbitrary", "arbitrary"),
        name="proj_g",
    )(h3, w, cw)


def _gates_kernel(h_ref, w_ref, bias_ref, alog_ref, col_ref, row_ref, carry_scr):
    s = pl.program_id(1)
    y = _dot(h_ref[...], w_ref[...])
    tm = y.shape[0]
    lane = lax.broadcasted_iota(jnp.int32, (1, LANES), 1)
    is_f = lane < N_HEADS
    is_a = (lane >= N_HEADS) & (lane < 2 * N_HEADS)
    yb = y + bias_ref[...]
    sp = _softplus(jnp.where(is_f, -yb, yb))
    val = jnp.where(is_f, -sp, jnp.where(is_a, -jnp.exp(alog_ref[...]) * sp, _sigmoid(y)))

    @pl.when(s == 0)
    def _():
        carry_scr[...] = jnp.zeros_like(carry_scr)

    rowi = lax.broadcasted_iota(jnp.int32, (tm, LANES), 0)
    rowc = rowi % GDN_CHUNK
    cum = val
    cumc = val
    sh = 1
    while sh < tm:
        cum = cum + jnp.where(rowi >= sh, pltpu.roll(cum, sh, 0), 0.0)
        if sh < GDN_CHUNK:
            cumc = cumc + jnp.where(rowc >= sh, pltpu.roll(cumc, sh, 0), 0.0)
        sh *= 2
    cum = cum + carry_scr[0:1, :]
    carry_scr[...] = jnp.broadcast_to(cum[tm - 1:tm, :], carry_scr.shape)
    out = jnp.where(is_f, cum * LOG2E, jnp.where(is_a, cumc, val))
    col_ref[0] = out
    row_ref[0] = out.T[:row_ref.shape[1], :]


def _gates(h3, w, bias, alog, tm):
    b, s, d = h3.shape
    nrow = 32
    return pl.pallas_call(
        _gates_kernel,
        grid=(b, s // tm),
        in_specs=[pl.BlockSpec((None, tm, d), lambda bb, i: (bb, i, 0)),
                  pl.BlockSpec((d, LANES), lambda bb, i: (0, 0)),
                  pl.BlockSpec((1, LANES), lambda bb, i: (0, 0)),
                  pl.BlockSpec((1, LANES), lambda bb, i: (0, 0))],
        out_specs=(pl.BlockSpec((1, tm, LANES), lambda bb, i: (bb, i, 0)),
                   pl.BlockSpec((1, nrow, tm), lambda bb, i: (bb, 0, i))),
        out_shape=(jax.ShapeDtypeStruct((b, s, LANES), F32),
                   jax.ShapeDtypeStruct((b, nrow, s), F32)),
        scratch_shapes=[pltpu.VMEM((SUBLANES, LANES), F32)],
        compiler_params=_params("arbitrary", "arbitrary"),
        name="gates",
    )(h3, w, bias, alog)


def _pick_lane(x, idx):
    lane = lax.broadcasted_iota(jnp.int32, x.shape, 1)
    return jnp.sum(jnp.where(lane == idx, x, 0.0), axis=-1, keepdims=True)


def _attn_kernel(tq, rg, tk, q_ref, k_ref, v_ref, cq_ref, ck_ref, gn_ref, o_ref, m_scr, acc_scr):
    h = pl.program_id(1)
    qi = pl.program_id(2)
    ng = tq // rg
    ahead = 3
    cq_all = _pick_lane(cq_ref[0], h)
    qs = [q_ref[0, g * rg:(g + 1) * rg, :] for g in range(ng)]
    cqs = [cq_all[g * rg:(g + 1) * rg] for g in range(ng)]
    q0 = pl.multiple_of(qi * tq, tq)

    def step(off, widths, first):
        wmax = max(widths)
        k = k_ref[0, pl.ds(off, wmax), :]
        v1 = jnp.concatenate([v_ref[0, pl.ds(off, wmax), :], jnp.ones((wmax, HEAD_DIM), BF16)], axis=1)
        ck = ck_ref[0, 0, :, pl.ds(off, wmax)]
        ss = {g: _dot_nt(qs[g], k[:widths[g]]) for g in range(min(ahead, ng))}
        for g in range(ng):
            w = widths[g]
            rows = slice(g * rg, (g + 1) * rg)
            if g + ahead < ng:
                ss[g + ahead] = _dot_nt(qs[g + ahead], k[:widths[g + ahead]])
            s = ss.pop(g) - ck[:, :w]
            if first:
                r = lax.broadcasted_iota(jnp.int32, (rg, w), 0) + g * rg
                c = lax.broadcasted_iota(jnp.int32, (rg, w), 1)
                s = jnp.where(c <= r, s, -jnp.inf)
                m_new = jnp.max(s, axis=-1, keepdims=True) + cqs[g]
            else:
                m_old = m_scr[rows, :]
                m_new = jnp.maximum(m_old, jnp.max(s, axis=-1, keepdims=True) + cqs[g])
            p = jnp.exp2(s - (m_new - cqs[g]))
            pv = _dot(p.astype(BF16), v1[:w])
            acc_scr[rows, :] = pv if first else jnp.exp2(m_old - m_new) * acc_scr[rows, :] + pv
            m_scr[rows, :] = m_new

    step(q0, [(g + 1) * rg for g in range(ng)], True)

    def body(j, carry):
        step(pl.multiple_of(j * tk, tk), [tk] * ng, False)
        return carry

    lax.fori_loop(0, qi * (tq // tk), body, 0)
    for g in range(ng):
        rows = slice(g * rg, (g + 1) * rg)
        o = acc_scr[rows, :HEAD_DIM] / acc_scr[rows, HEAD_DIM:]
        o_ref[0, rows, :] = (_rms(o) * gn_ref[...]).astype(o_ref.dtype)


def _attn(qkv, col, row, gn, tq, rg, tk):
    b, s, _ = qkv.shape
    return pl.pallas_call(
        functools.partial(_attn_kernel, tq, rg, tk),
        grid=(b, N_HEADS, s // tq),
        in_specs=[pl.BlockSpec((1, tq, HEAD_DIM), lambda bb, h, i: (bb, i, h)),
                  pl.BlockSpec((1, s, HEAD_DIM), lambda bb, h, i: (bb, 0, N_HEADS + h)),
                  pl.BlockSpec((1, s, HEAD_DIM), lambda bb, h, i: (bb, 0, 2 * N_HEADS + h)),
                  pl.BlockSpec((1, tq, LANES), lambda bb, h, i: (bb, i, 0)),
                  pl.BlockSpec((1, 1, 1, s), lambda bb, h, i: (bb, h, 0, 0)),
                  pl.BlockSpec((1, HEAD_DIM), lambda bb, h, i: (0, 0))],
        out_specs=pl.BlockSpec((1, tq, HEAD_DIM), lambda bb, h, i: (bb, i, h)),
        out_shape=jax.ShapeDtypeStruct((b, s, GROUP_W), BF16),
        scratch_shapes=[pltpu.VMEM((tq, 1), F32), pltpu.VMEM((tq, 2 * HEAD_DIM), F32)],
        compiler_params=_params("arbitrary", "arbitrary", "arbitrary"),
        name="attn",
    )(qkv, qkv, qkv, col, row, gn)


def _unit_lower_inverses(lowers):
    c = lowers[0].shape[0]
    r = lax.broadcasted_iota(jnp.int32, (c, c), 0)
    cc = lax.broadcasted_iota(jnp.int32, (c, c), 1)
    eye = jnp.where(r == cc, 1.0, 0.0)
    pair = (r ^ cc) == 1
    invs = [eye - jnp.where(pair, low, 0.0) for low in lowers]
    level = 1
    while (1 << level) < c:
        sel = ((r >> level) ^ (cc >> level)) == 1
        inv_b = [inv.astype(BF16) for inv in invs]
        half = [_dot(ib, jnp.where(sel, low, 0.0).astype(BF16)).astype(BF16) for ib, low in zip(inv_b, lowers)]
        invs = [inv - _dot(hf, ib) for inv, hf, ib in zip(invs, half, inv_b)]
        level += 1
    return invs


def _gdn_kernel(t_tokens, q_ref, k_ref, v_ref, z_ref, col_ref, row_ref, gn_ref, o_ref, state_scr):
    t = pl.program_id(1)
    c = GDN_CHUNK
    n_chunks = t_tokens // c

    @pl.when(t == 0)
    def _():
        state_scr[...] = jnp.zeros_like(state_scr)

    col = col_ref[0]
    r = lax.broadcasted_iota(jnp.int32, (c, c), 0)
    cc = lax.broadcasted_iota(jnp.int32, (c, c), 1)
    incl = cc <= r
    strict = cc < r
    chains = [(hh, ci) for ci in range(n_chunks) for hh in range(N_HEADS)]
    g_all = [_pick_lane(col, N_HEADS + hh) for hh in range(N_HEADS)]
    b_all = [_pick_lane(col, 2 * N_HEADS + hh) for hh in range(N_HEADS)]

    def tile(ref, hh, ci):
        return ref[0, ci * c:(ci + 1) * c, hh * HEAD_DIM:(hh + 1) * HEAD_DIM]

    lowers, a_intra = [], []
    for hh, ci in chains:
        k = tile(k_ref, hh, ci)
        g = g_all[hh][ci * c:(ci + 1) * c]
        g_row = row_ref[0, hh, :, ci * c:(ci + 1) * c]
        decay = jnp.exp(jnp.where(incl, g - g_row, -jnp.inf))
        k_b = k.astype(BF16)
        kb_b = (k * b_all[hh][ci * c:(ci + 1) * c]).astype(BF16)
        lowers.append(jnp.where(strict, _dot_nt(kb_b, k_b) * decay, 0.0))
        a_intra.append(jnp.where(incl, _dot_nt(tile(q_ref, hh, ci).astype(BF16), k_b) * decay, 0.0).astype(BF16))
    t_invs = [inv.astype(BF16) for inv in _unit_lower_inverses(lowers)]
    us, ws = [], []
    for (hh, ci), t_inv in zip(chains, t_invs):
        g = g_all[hh][ci * c:(ci + 1) * c]
        beta = b_all[hh][ci * c:(ci + 1) * c]
        us.append(_dot(t_inv, (tile(v_ref, hh, ci) * beta).astype(BF16)))
        ws.append(_dot(t_inv, (tile(k_ref, hh, ci) * (beta * jnp.exp(g))).astype(BF16)).astype(BF16))

    states = [state_scr[hh] for hh in range(N_HEADS)]
    for ci in range(n_chunks):
        idxs = [chains.index((hh, ci)) for hh in range(N_HEADS)]
        gs = [g_all[hh][ci * c:(ci + 1) * c] for hh in range(N_HEADS)]
        state_bs = [st.astype(BF16) for st in states]
        w_s = [_dot(ws[i], sb) for i, sb in zip(idxs, state_bs)]
        q_s = [_dot((tile(q_ref, hh, ci) * jnp.exp(gs[hh])).astype(BF16), state_bs[hh]) for hh in range(N_HEADS)]
        v_new_bs = [(us[i] - w_s[hh]).astype(BF16) for hh, i in enumerate(idxs)]
        for hh, i in enumerate(idxs):
            g = gs[hh]
            g_last = g[c - 1:c, :]
            o = q_s[hh] + _dot(a_intra[i], v_new_bs[hh])
            k_dec = (tile(k_ref, hh, ci) * jnp.exp(g_last - g)).astype(BF16)
            states[hh] = states[hh] * jnp.exp(g_last) + _dot_tn(k_dec, v_new_bs[hh])
            o_ref[0, ci * c:(ci + 1) * c, hh * HEAD_DIM:(hh + 1) * HEAD_DIM] = (
                _rms(o) * gn_ref[...] * tile(z_ref, hh, ci)).astype(o_ref.dtype)
    for hh in range(N_HEADS):
        state_scr[hh] = states[hh]


def _gdn(qkv, z, col, row, gn, t_tokens):
    b, s, _ = qkv.shape
    group = lambda g: pl.BlockSpec((1, t_tokens, GROUP_W), lambda bb, i: (bb, i, g))
    return pl.pallas_call(
        functools.partial(_gdn_kernel, t_tokens),
        grid=(b, s // t_tokens),
        in_specs=[group(0), group(1), group(2), group(0),
                  pl.BlockSpec((1, t_tokens, LANES), lambda bb, i: (bb, i, 0)),
                  pl.BlockSpec((1, N_HEADS, 1, t_tokens), lambda bb, i: (bb, 1, 0, i)),
                  pl.BlockSpec((1, HEAD_DIM), lambda bb, i: (0, 0))],
        out_specs=pl.BlockSpec((1, t_tokens, GROUP_W), lambda bb, i: (bb, i, 0)),
        out_shape=jax.ShapeDtypeStruct((b, s, GROUP_W), BF16),
        scratch_shapes=[pltpu.VMEM((N_HEADS, HEAD_DIM, HEAD_DIM), F32)],
        compiler_params=_params("arbitrary", "arbitrary"),
        name="gdn",
    )(qkv, qkv, qkv, z, col, row, gn)


def _oproj_kernel(of_ref, og_ref, wf_ref, wg_ref, x_ref, gt_ref, o_ref):
    y = _dot(of_ref[...], wf_ref[...]) + _dot(og_ref[...], wg_ref[...])
    o_ref[...] = x_ref[...] + gt_ref[0] * y


def _oproj(o_f, o_g, w, x, mod, tiles_per_batch, tm):
    n, d = x.shape
    row = lambda i: (i, 0)
    return pl.pallas_call(
        _oproj_kernel,
        grid=(n // tm,),
        in_specs=[pl.BlockSpec((tm, GROUP_W), row), pl.BlockSpec((tm, GROUP_W), row),
                  pl.BlockSpec((GROUP_W, d), lambda i: (0, 0)), pl.BlockSpec((GROUP_W, d), lambda i: (1, 0)),
                  pl.BlockSpec((tm, d), row),
                  pl.BlockSpec((1, 1, d), lambda i: ((i // tiles_per_batch) * N_MOD + 5, 0, 0))],
        out_specs=pl.BlockSpec((tm, d), row),
        out_shape=jax.ShapeDtypeStruct((n, d), F32),
        compiler_params=_params("arbitrary"),
        name="oproj",
    )(o_f, o_g, w, w, x, mod)


def kernel(x, c, ada_w, ada_b, norm_g, ffn_w_gate, ffn_w_up, ffn_w_down, w_in, w_out,
           fox_f_bias, fox_out_norm, gdn_conv, gdn_A_log, gdn_dt_bias, gdn_out_norm, final_norm):
    bsz, s, d = x.shape
    n = bsz * s
    tm = 512
    tiles_per_batch = s // tm
    l = 0
    bf = lambda t: t.astype(BF16)

    c_pad = jnp.zeros((SUBLANES, d), F32).at[:bsz].set(c)
    mod9 = _ada(c_pad, ada_w, ada_b, l)[:bsz].reshape(bsz, N_MOD, d)
    mod = mod9.reshape(bsz * N_MOD, 1, d)
    rep = lambda v: jnp.broadcast_to(v[None, None, :], (bsz, 1, d))
    zero = jnp.zeros((bsz, 1, d), F32)
    vecs1 = jnp.concatenate([rep(norm_g[l, 0]), mod9[:, 0:3], rep(norm_g[l, 1]), mod9[:, 3:5], zero], axis=1)
    vecs2 = jnp.concatenate([rep(norm_g[l, 2]), mod9[:, 6:9], rep(final_norm), zero, zero, zero], axis=1)

    ffn_w = (bf(ffn_w_gate[l]), bf(ffn_w_up[l]), bf(ffn_w_down[l]))

    xf = x.reshape(n, d)
    x1, h2 = _ffn(xf, vecs1, tiles_per_batch, tm, *ffn_w, 0, final=False)

    w = bf(w_in[l])
    qkv_f = _proj(h2, w, 3 * GROUP_W, 1024, False, BF16, "proj_f").reshape(bsz, s, 3 * GROUP_W)
    z_g = _proj(h2, w[:, _OFF_ZG:], GROUP_W, 1024, True, F32, "proj_z").reshape(bsz, s, GROUP_W)
    h3 = h2.reshape(bsz, s, d)
    cw = jnp.zeros((SUBLANES, 3 * GROUP_W), F32).at[:CONV_W].set(gdn_conv[l])
    qkv_g = _proj_g(h3, w[:, _OFF_QKVG:_OFF_AG], cw, min(1024, s))

    pad = LANES - 3 * N_HEADS
    w_s = jnp.concatenate([w[:, _OFF_FF:_OFF_QKVG], w[:, _OFF_AG:_OFF_ZG], jnp.zeros((d, pad), BF16)], axis=1)
    zeros8 = jnp.zeros((N_HEADS,), F32)
    bias = jnp.concatenate([fox_f_bias[l], gdn_dt_bias[l], zeros8, jnp.zeros((pad,), F32)])[None]
    alog = jnp.concatenate([zeros8, gdn_A_log[l], zeros8, jnp.zeros((pad,), F32)])[None]
    col, row = _gates(h3, w_s, bias, alog, tm)
    row = row.reshape(bsz, row.shape[1], 1, s)

    o_f = _attn(qkv_f, col, row, fox_out_norm[l][None], min(4096, s), 256, 1024)
    o_g = _gdn(qkv_g, z_g, col, row, gdn_out_norm[l][None], 256)

    x2 = _oproj(o_f.reshape(n, GROUP_W), o_g.reshape(n, GROUP_W), bf(w_out[l]), x1, mod, tiles_per_batch, tm)

    out = _ffn(x2, vecs2, tiles_per_batch, tm, *ffn_w, 1, final=True)
    return out.reshape(bsz, s, d)
```

```python
import functools
import math

import jax
import jax.numpy as jnp
from jax import lax
from jax.experimental import pallas as pl
from jax.experimental.pallas import tpu as pltpu

F32 = jnp.float32
BF16 = jnp.bfloat16

D_MODEL = 2048
HEAD_DIM = 128
N_HEADS = 8
GROUP_W = N_HEADS * HEAD_DIM
D_FF = 5632
CONV_W = 4
N_MOD = 9
EPS = 1e-6
LANES = 128
SUBLANES = 8
GDN_CHUNK = 128
LOG2E = math.log2(math.e)

_OFF_QF = 0
_OFF_FF = 3 * GROUP_W
_OFF_QKVG = _OFF_FF + N_HEADS
_OFF_AG = _OFF_QKVG + 3 * GROUP_W
_OFF_BG = _OFF_AG + N_HEADS
_OFF_ZG = _OFF_BG + N_HEADS

VMEM_LIMIT = 56 * 1024 * 1024


def _params(*sem):
    return pltpu.CompilerParams(dimension_semantics=sem, vmem_limit_bytes=VMEM_LIMIT)


def _sigmoid(x):
    return 1.0 / (1.0 + jnp.exp(-x))


def _silu(x):
    return x * _sigmoid(x)


def _softplus(x):
    return jnp.maximum(x, 0.0) + jnp.log1p(jnp.exp(-jnp.abs(x)))


def _dot(a, b):
    return jnp.dot(a, b, preferred_element_type=F32)


def _dot_nt(a, b):
    return lax.dot_general(a, b, (((1,), (1,)), ((), ())), preferred_element_type=F32)


def _dot_tn(a, b):
    return lax.dot_general(a, b, (((0,), (0,)), ((), ())), preferred_element_type=F32)


def _rms(x):
    return x * lax.rsqrt(jnp.mean(x * x, axis=-1, keepdims=True) + EPS)


def _ada_kernel(c_ref, w_ref, b_ref, o_ref):
    cond = _silu(c_ref[...])
    o_ref[...] = _dot(cond.astype(BF16), w_ref[...].astype(BF16)) + b_ref[...]


def _ada(c_pad, w, b, layer):
    rows, d = c_pad.shape
    n = w.shape[2]
    tn = 1024
    return pl.pallas_call(
        _ada_kernel,
        grid=(n // tn,),
        in_specs=[pl.BlockSpec((rows, d), lambda j: (0, 0)),
                  pl.BlockSpec((None, d, tn), lambda j: (layer, 0, j)),
                  pl.BlockSpec((1, tn), lambda j: (layer, j))],
        out_specs=pl.BlockSpec((rows, tn), lambda j: (0, j)),
        out_shape=jax.ShapeDtypeStruct((rows, n), F32),
        compiler_params=_params("arbitrary"),
        name="ada",
    )(c_pad, w, b)


_ROW_CHUNK = 64


def _inv_rms(x):
    return lax.rsqrt(jnp.mean(x * x, axis=-1, keepdims=True) + EPS)


def _ffn_kernel(final, x_ref, vec_ref, wg_ref, wu_ref, wd_ref, *rest):
    if final:
        o_ref, h_scr, acc_scr, r_scr = rest
    else:
        o_ref, h2_ref, h_scr, acc_scr, r_scr = rest
    f = pl.program_id(1)
    last = pl.num_programs(1) - 1
    tm = x_ref.shape[0]
    vec = lambda r: vec_ref[0, r:r + 1, :]
    halves = [slice(0, tm // 2), slice(tm // 2, tm)]

    def chunks(rows):
        return [slice(r, r + _ROW_CHUNK) for r in range(rows.start, rows.stop, _ROW_CHUNK)]

    def swiglu_down(rows):
        h = h_scr[rows, :]
        act = _silu(_dot(h, wg_ref[...])) * _dot(h, wu_ref[...])
        return _dot(act.astype(BF16), wd_ref[...])

    @pl.when(f == 0)
    def _():
        gain = vec(0) * (1.0 + vec(2))
        shift = vec(1)
        for rows in halves:
            for c in chunks(rows):
                r_scr[c, :] = _inv_rms(x_ref[c, :])
            for c in chunks(rows):
                h_scr[c, :] = (x_ref[c, :] * r_scr[c, :] * gain + shift).astype(BF16)
            acc_scr[rows, :] = swiglu_down(rows)

    @pl.when((f > 0) & (f < last))
    def _():
        acc_scr[...] += swiglu_down(slice(0, tm))

    @pl.when(f == last)
    def _():
        half_gate = 0.5 * vec(3)
        gain2 = vec(4) if final else vec(4) * (1.0 + vec(6))
        for rows in halves:
            acc_scr[rows, :] += swiglu_down(rows)
            for c in chunks(rows):
                xn = x_ref[c, :] + half_gate * acc_scr[c, :]
                o_ref[c, :] = xn
                r_scr[c, :] = _inv_rms(xn)
            for c in chunks(rows):
                y = o_ref[c, :] * r_scr[c, :] * gain2
                if final:
                    o_ref[c, :] = y
                else:
                    h2_ref[c, :] = (y + vec(5)).astype(BF16)


_FFN_TF = 512


def _ffn(x, vecs, tiles_per_batch, tm, wg, wu, wd, which, final):
    n, d = x.shape
    dff = wd.shape[1]
    tf = _FFN_TF
    row = lambda i, f: (i, 0)
    in_specs = [pl.BlockSpec((tm, d), row),
                pl.BlockSpec((1, SUBLANES, d), lambda i, f: (i // tiles_per_batch, 0, 0)),
                pl.BlockSpec((None, d, tf), lambda i, f: (which, 0, f)),
                pl.BlockSpec((None, d, tf), lambda i, f: (which, 0, f)),
                pl.BlockSpec((None, tf, d), lambda i, f: (which, f, 0))]
    if final:
        out_shape = jax.ShapeDtypeStruct((n, d), F32)
        out_specs = pl.BlockSpec((tm, d), row)
    else:
        out_shape = (jax.ShapeDtypeStruct((n, d), F32), jax.ShapeDtypeStruct((n, d), BF16))
        out_specs = (pl.BlockSpec((tm, d), row), pl.BlockSpec((tm, d), row))
    return pl.pallas_call(
        functools.partial(_ffn_kernel, final),
        grid=(n // tm, dff // tf),
        in_specs=in_specs,
        out_specs=out_specs,
        out_shape=out_shape,
        scratch_shapes=[pltpu.VMEM((tm, d), BF16), pltpu.VMEM((tm, d), F32), pltpu.VMEM((tm, 1), F32)],
        compiler_params=_params("arbitrary", "arbitrary"),
        name="ffn_final" if final else "ffn",
    )(x, vecs, wg, wu, wd)


_CONV_COLS = 2 * HEAD_DIM


def _proj_g_kernel(h_ref, w_ref, wf_ref, cw_ref, o_ref, of_ref, prev_scr, *bufs):
    j = pl.program_id(0)
    s = pl.program_id(2)
    tm = h_ref.shape[0]
    h = h_ref[...]
    cw = cw_ref[...]
    prev = jnp.where(s == 0, 0.0, prev_scr[...])
    gain = jnp.where(j == 0, HEAD_DIM ** -0.5, 1.0)
    for p, buf in enumerate(bufs):
        cols = slice(p * _CONV_COLS, (p + 1) * _CONV_COLS)
        y = _dot(h, w_ref[:, cols])
        buf[0:SUBLANES, :] = prev[:, cols]
        buf[SUBLANES:, :] = y
        prev_scr[:, cols] = y[tm - SUBLANES:, :]
        out = y * cw[CONV_W - 1:CONV_W, cols]
        for sh in range(1, CONV_W):
            out = out + buf[SUBLANES - sh:SUBLANES - sh + tm, :] * cw[CONV_W - 1 - sh:CONV_W - sh, cols]
        out = _silu(out)
        for hh in range(_CONV_COLS // HEAD_DIM):
            t = out[:, hh * HEAD_DIM:(hh + 1) * HEAD_DIM]
            nrm = lax.rsqrt(jnp.sum(t * t, axis=-1, keepdims=True) + EPS) * gain
            c0 = p * _CONV_COLS + hh * HEAD_DIM
            o_ref[0, :, c0:c0 + HEAD_DIM] = t * jnp.where(j < 2, nrm, 1.0)
        yf = _dot(h, wf_ref[:, cols]) * jnp.where(j == 0, LOG2E / math.sqrt(HEAD_DIM), 1.0)
        of_ref[0, :, cols] = yf.astype(of_ref.dtype)


def _proj_g(h3, w, wf, cw, tm):
    b, s, d = h3.shape
    c = w.shape[1]
    tn = GROUP_W
    n_pieces = tn // _CONV_COLS
    return pl.pallas_call(
        _proj_g_kernel,
        grid=(c // tn, b, s // tm),
        in_specs=[pl.BlockSpec((None, tm, d), lambda j, bb, i: (bb, i, 0)),
                  pl.BlockSpec((d, tn), lambda j, bb, i: (0, j)),
                  pl.BlockSpec((d, tn), lambda j, bb, i: (0, j)),
                  pl.BlockSpec((SUBLANES, tn), lambda j, bb, i: (0, j))],
        out_specs=(pl.BlockSpec((1, tm, tn), lambda j, bb, i: (bb, i, j)),
                   pl.BlockSpec((1, tm, tn), lambda j, bb, i: (bb, i, j))),
        out_shape=(jax.ShapeDtypeStruct((b, s, c), F32), jax.ShapeDtypeStruct((b, s, c), BF16)),
        scratch_shapes=[pltpu.VMEM((SUBLANES, tn), F32)]
        + [pltpu.VMEM((tm + SUBLANES, _CONV_COLS), F32) for _ in range(n_pieces)],
        compiler_params=_params("arbitrary", "arbitrary", "arbitrary"),
        name="proj_g",
    )(h3, w, wf, cw)


def _gates_kernel(h_ref, w_ref, wz_ref, bias_ref, alog_ref, col_ref, row_ref, z_ref, carry_scr):
    s = pl.program_id(1)
    h = h_ref[...]
    z_ref[0] = _silu(_dot(h, wz_ref[...]))
    y = _dot(h, w_ref[...])
    tm = y.shape[0]
    lane = lax.broadcasted_iota(jnp.int32, (1, LANES), 1)
    is_f = lane < N_HEADS
    is_a = (lane >= N_HEADS) & (lane < 2 * N_HEADS)
    yb = y + bias_ref[...]
    sp = _softplus(jnp.where(is_f, -yb, yb))
    val = jnp.where(is_f, -sp, jnp.where(is_a, -jnp.exp(alog_ref[...]) * sp, _sigmoid(y)))

    @pl.when(s == 0)
    def _():
        carry_scr[...] = jnp.zeros_like(carry_scr)

    rowi = lax.broadcasted_iota(jnp.int32, (tm, LANES), 0)
    rowc = rowi % GDN_CHUNK
    cum = val
    cumc = val
    sh = 1
    while sh < tm:
        cum = cum + jnp.where(rowi >= sh, pltpu.roll(cum, sh, 0), 0.0)
        if sh < GDN_CHUNK:
            cumc = cumc + jnp.where(rowc >= sh, pltpu.roll(cumc, sh, 0), 0.0)
        sh *= 2
    cum = cum + carry_scr[0:1, :]
    carry_scr[...] = jnp.broadcast_to(cum[tm - 1:tm, :], carry_scr.shape)
    out = jnp.where(is_f, cum * LOG2E, jnp.where(is_a, cumc, val))
    col_ref[0] = out
    row_ref[0] = out.T[:row_ref.shape[1], :]


def _gates(h3, w, wz, bias, alog, tm):
    b, s, d = h3.shape
    nrow = 32
    cz = wz.shape[1]
    return pl.pallas_call(
        _gates_kernel,
        grid=(b, s // tm),
        in_specs=[pl.BlockSpec((None, tm, d), lambda bb, i: (bb, i, 0)),
                  pl.BlockSpec((d, LANES), lambda bb, i: (0, 0)),
                  pl.BlockSpec((d, cz), lambda bb, i: (0, 0)),
                  pl.BlockSpec((1, LANES), lambda bb, i: (0, 0)),
                  pl.BlockSpec((1, LANES), lambda bb, i: (0, 0))],
        out_specs=(pl.BlockSpec((1, tm, LANES), lambda bb, i: (bb, i, 0)),
                   pl.BlockSpec((1, nrow, tm), lambda bb, i: (bb, 0, i)),
                   pl.BlockSpec((1, tm, cz), lambda bb, i: (bb, i, 0))),
        out_shape=(jax.ShapeDtypeStruct((b, s, LANES), F32),
                   jax.ShapeDtypeStruct((b, nrow, s), F32),
                   jax.ShapeDtypeStruct((b, s, cz), F32)),
        scratch_shapes=[pltpu.VMEM((SUBLANES, LANES), F32)],
        compiler_params=_params("arbitrary", "arbitrary"),
        name="gates",
    )(h3, w, wz, bias, alog)


def _pick_lane(x, idx):
    lane = lax.broadcasted_iota(jnp.int32, x.shape, 1)
    return jnp.sum(jnp.where(lane == idx, x, 0.0), axis=-1, keepdims=True)


def _attn_kernel(tq, rg, tk, q_ref, k_ref, v_ref, cq_ref, ck_ref, gn_ref, o_ref, m_scr, acc_scr):
    h = pl.program_id(1)
    qi = pl.program_id(2)
    ng = tq // rg
    ahead = 3
    cq_all = _pick_lane(cq_ref[0], h)
    qs = [q_ref[0, g * rg:(g + 1) * rg, :] for g in range(ng)]
    cqs = [cq_all[g * rg:(g + 1) * rg] for g in range(ng)]
    q0 = pl.multiple_of(qi * tq, tq)

    def step(off, widths, first):
        wmax = max(widths)
        k = k_ref[0, pl.ds(off, wmax), :]
        v1 = jnp.concatenate([v_ref[0, pl.ds(off, wmax), :], jnp.ones((wmax, HEAD_DIM), BF16)], axis=1)
        ck = ck_ref[0, 0, :, pl.ds(off, wmax)]
        ss = {g: _dot_nt(qs[g], k[:widths[g]]) for g in range(min(ahead, ng))}
        for g in range(ng):
            w = widths[g]
            rows = slice(g * rg, (g + 1) * rg)
            if g + ahead < ng:
                ss[g + ahead] = _dot_nt(qs[g + ahead], k[:widths[g + ahead]])
            s = ss.pop(g) - ck[:, :w]
            if first:
                r = lax.broadcasted_iota(jnp.int32, (rg, w), 0) + g * rg
                c = lax.broadcasted_iota(jnp.int32, (rg, w), 1)
                s = jnp.where(c <= r, s, -jnp.inf)
                m_new = jnp.max(s, axis=-1, keepdims=True) + cqs[g]
            else:
                m_old = m_scr[rows, :]
                m_new = jnp.maximum(m_old, jnp.max(s, axis=-1, keepdims=True) + cqs[g])
            p = jnp.exp2(s - (m_new - cqs[g]))
            pv = _dot(p.astype(BF16), v1[:w])
            acc_scr[rows, :] = pv if first else jnp.exp2(m_old - m_new) * acc_scr[rows, :] + pv
            m_scr[rows, :] = m_new

    step(q0, [(g + 1) * rg for g in range(ng)], True)

    def body(j, carry):
        step(pl.multiple_of(j * tk, tk), [tk] * ng, False)
        return carry

    lax.fori_loop(0, qi * (tq // tk), body, 0)
    for g in range(ng):
        rows = slice(g * rg, (g + 1) * rg)
        o = acc_scr[rows, :HEAD_DIM] / acc_scr[rows, HEAD_DIM:]
        o_ref[0, rows, :] = (_rms(o) * gn_ref[...]).astype(o_ref.dtype)


def _attn(qkv, col, row, gn, tq, rg, tk):
    b, s, _ = qkv.shape
    return pl.pallas_call(
        functools.partial(_attn_kernel, tq, rg, tk),
        grid=(b, N_HEADS, s // tq),
        in_specs=[pl.BlockSpec((1, tq, HEAD_DIM), lambda bb, h, i: (bb, i, h)),
                  pl.BlockSpec((1, s, HEAD_DIM), lambda bb, h, i: (bb, 0, N_HEADS + h)),
                  pl.BlockSpec((1, s, HEAD_DIM), lambda bb, h, i: (bb, 0, 2 * N_HEADS + h)),
                  pl.BlockSpec((1, tq, LANES), lambda bb, h, i: (bb, i, 0)),
                  pl.BlockSpec((1, 1, 1, s), lambda bb, h, i: (bb, h, 0, 0)),
                  pl.BlockSpec((1, HEAD_DIM), lambda bb, h, i: (0, 0))],
        out_specs=pl.BlockSpec((1, tq, HEAD_DIM), lambda bb, h, i: (bb, i, h)),
        out_shape=jax.ShapeDtypeStruct((b, s, GROUP_W), BF16),
        scratch_shapes=[pltpu.VMEM((tq, 1), F32), pltpu.VMEM((tq, 2 * HEAD_DIM), F32)],
        compiler_params=_params("arbitrary", "arbitrary", "arbitrary"),
        name="attn",
    )(qkv, qkv, qkv, col, row, gn)


def _unit_lower_inverses(lowers):
    c = lowers[0].shape[0]
    r = lax.broadcasted_iota(jnp.int32, (c, c), 0)
    cc = lax.broadcasted_iota(jnp.int32, (c, c), 1)
    eye = jnp.where(r == cc, 1.0, 0.0)
    pair = (r ^ cc) == 1
    invs = [eye - jnp.where(pair, low, 0.0) for low in lowers]
    level = 1
    while (1 << level) < c:
        sel = ((r >> level) ^ (cc >> level)) == 1
        inv_b = [inv.astype(BF16) for inv in invs]
        half = [_dot(ib, jnp.where(sel, low, 0.0).astype(BF16)).astype(BF16) for ib, low in zip(inv_b, lowers)]
        invs = [inv - _dot(hf, ib) for inv, hf, ib in zip(invs, half, inv_b)]
        level += 1
    return invs


def _gdn_kernel(t_tokens, q_ref, k_ref, v_ref, z_ref, col_ref, row_ref, gn_ref, o_ref, state_scr):
    t = pl.program_id(1)
    c = GDN_CHUNK
    n_chunks = t_tokens // c

    @pl.when(t == 0)
    def _():
        state_scr[...] = jnp.zeros_like(state_scr)

    col = col_ref[0]
    r = lax.broadcasted_iota(jnp.int32, (c, c), 0)
    cc = lax.broadcasted_iota(jnp.int32, (c, c), 1)
    incl = cc <= r
    strict = cc < r
    chains = [(hh, ci) for ci in range(n_chunks) for hh in range(N_HEADS)]
    g_all = [_pick_lane(col, N_HEADS + hh) for hh in range(N_HEADS)]
    b_all = [_pick_lane(col, 2 * N_HEADS + hh) for hh in range(N_HEADS)]

    def tile(ref, hh, ci):
        return ref[0, ci * c:(ci + 1) * c, hh * HEAD_DIM:(hh + 1) * HEAD_DIM]

    lowers, a_intra = [], []
    for hh, ci in chains:
        k = tile(k_ref, hh, ci)
        g = g_all[hh][ci * c:(ci + 1) * c]
        g_row = row_ref[0, hh, :, ci * c:(ci + 1) * c]
        decay = jnp.exp(jnp.where(incl, g - g_row, -jnp.inf))
        k_b = k.astype(BF16)
        kb_b = (k * b_all[hh][ci * c:(ci + 1) * c]).astype(BF16)
        lowers.append(jnp.where(strict, _dot_nt(kb_b, k_b) * decay, 0.0))
        a_intra.append(jnp.where(incl, _dot_nt(tile(q_ref, hh, ci).astype(BF16), k_b) * decay, 0.0).astype(BF16))
    t_invs = [inv.astype(BF16) for inv in _unit_lower_inverses(lowers)]
    us, ws = [], []
    for (hh, ci), t_inv in zip(chains, t_invs):
        g = g_all[hh][ci * c:(ci + 1) * c]
        beta = b_all[hh][ci * c:(ci + 1) * c]
        us.append(_dot(t_inv, (tile(v_ref, hh, ci) * beta).astype(BF16)))
        ws.append(_dot(t_inv, (tile(k_ref, hh, ci) * (beta * jnp.exp(g))).astype(BF16)).astype(BF16))

    states = [state_scr[hh] for hh in range(N_HEADS)]
    for ci in range(n_chunks):
        idxs = [chains.index((hh, ci)) for hh in range(N_HEADS)]
        gs = [g_all[hh][ci * c:(ci + 1) * c] for hh in range(N_HEADS)]
        state_bs = [st.astype(BF16) for st in states]
        w_s = [_dot(ws[i], sb) for i, sb in zip(idxs, state_bs)]
        q_s = [_dot((tile(q_ref, hh, ci) * jnp.exp(gs[hh])).astype(BF16), state_bs[hh]) for hh in range(N_HEADS)]
        v_new_bs = [(us[i] - w_s[hh]).astype(BF16) for hh, i in enumerate(idxs)]
        for hh, i in enumerate(idxs):
            g = gs[hh]
            g_last = g[c - 1:c, :]
            o = q_s[hh] + _dot(a_intra[i], v_new_bs[hh])
            k_dec = (tile(k_ref, hh, ci) * jnp.exp(g_last - g)).astype(BF16)
            states[hh] = states[hh] * jnp.exp(g_last) + _dot_tn(k_dec, v_new_bs[hh])
            o_ref[0, ci * c:(ci + 1) * c, hh * HEAD_DIM:(hh + 1) * HEAD_DIM] = (
                _rms(o) * gn_ref[...] * tile(z_ref, hh, ci)).astype(o_ref.dtype)
    for hh in range(N_HEADS):
        state_scr[hh] = states[hh]


def _gdn(qkv, z, col, row, gn, t_tokens):
    b, s, _ = qkv.shape
    group = lambda g: pl.BlockSpec((1, t_tokens, GROUP_W), lambda bb, i: (bb, i, g))
    return pl.pallas_call(
        functools.partial(_gdn_kernel, t_tokens),
        grid=(b, s // t_tokens),
        in_specs=[group(0), group(1), group(2), group(0),
                  pl.BlockSpec((1, t_tokens, LANES), lambda bb, i: (bb, i, 0)),
                  pl.BlockSpec((1, N_HEADS, 1, t_tokens), lambda bb, i: (bb, 1, 0, i)),
                  pl.BlockSpec((1, HEAD_DIM), lambda bb, i: (0, 0))],
        out_specs=pl.BlockSpec((1, t_tokens, GROUP_W), lambda bb, i: (bb, i, 0)),
        out_shape=jax.ShapeDtypeStruct((b, s, GROUP_W), BF16),
        scratch_shapes=[pltpu.VMEM((N_HEADS, HEAD_DIM, HEAD_DIM), F32)],
        compiler_params=_params("arbitrary", "arbitrary"),
        name="gdn",
    )(qkv, qkv, qkv, z, col, row, gn)


def _oproj_kernel(of_ref, og_ref, wf_ref, wg_ref, x_ref, gt_ref, o_ref):
    y = _dot(of_ref[...], wf_ref[...]) + _dot(og_ref[...], wg_ref[...])
    o_ref[...] = x_ref[...] + gt_ref[0] * y


def _oproj(o_f, o_g, w, x, mod, tiles_per_batch, tm):
    n, d = x.shape
    row = lambda i: (i, 0)
    return pl.pallas_call(
        _oproj_kernel,
        grid=(n // tm,),
        in_specs=[pl.BlockSpec((tm, GROUP_W), row), pl.BlockSpec((tm, GROUP_W), row),
                  pl.BlockSpec((GROUP_W, d), lambda i: (0, 0)), pl.BlockSpec((GROUP_W, d), lambda i: (1, 0)),
                  pl.BlockSpec((tm, d), row),
                  pl.BlockSpec((1, 1, d), lambda i: ((i // tiles_per_batch) * N_MOD + 5, 0, 0))],
        out_specs=pl.BlockSpec((tm, d), row),
        out_shape=jax.ShapeDtypeStruct((n, d), F32),
        compiler_params=_params("arbitrary"),
        name="oproj",
    )(o_f, o_g, w, w, x, mod)


def kernel(x, c, ada_w, ada_b, norm_g, ffn_w_gate, ffn_w_up, ffn_w_down, w_in, w_out,
           fox_f_bias, fox_out_norm, gdn_conv, gdn_A_log, gdn_dt_bias, gdn_out_norm, final_norm):
    bsz, s, d = x.shape
    n = bsz * s
    tm = 512
    tiles_per_batch = s // tm
    l = 0
    bf = lambda t: t.astype(BF16)

    c_pad = jnp.zeros((SUBLANES, d), F32).at[:bsz].set(c)
    mod9 = _ada(c_pad, ada_w, ada_b, l)[:bsz].reshape(bsz, N_MOD, d)
    mod = mod9.reshape(bsz * N_MOD, 1, d)
    rep = lambda v: jnp.broadcast_to(v[None, None, :], (bsz, 1, d))
    zero = jnp.zeros((bsz, 1, d), F32)
    vecs1 = jnp.concatenate([rep(norm_g[l, 0]), mod9[:, 0:3], rep(norm_g[l, 1]), mod9[:, 3:5], zero], axis=1)
    vecs2 = jnp.concatenate([rep(norm_g[l, 2]), mod9[:, 6:9], rep(final_norm), zero, zero, zero], axis=1)

    ffn_w = (bf(ffn_w_gate[l]), bf(ffn_w_up[l]), bf(ffn_w_down[l]))

    xf = x.reshape(n, d)
    x1, h2 = _ffn(xf, vecs1, tiles_per_batch, tm, *ffn_w, 0, final=False)

    w = bf(w_in[l])
    h3 = h2.reshape(bsz, s, d)
    cw = jnp.zeros((SUBLANES, 3 * GROUP_W), F32).at[:CONV_W].set(gdn_conv[l])
    qkv_g, qkv_f = _proj_g(h3, w[:, _OFF_QKVG:_OFF_AG], w, cw, min(1024, s))

    pad = LANES - 3 * N_HEADS
    w_s = jnp.concatenate([w[:, _OFF_FF:_OFF_QKVG], w[:, _OFF_AG:_OFF_ZG], jnp.zeros((d, pad), BF16)], axis=1)
    zeros8 = jnp.zeros((N_HEADS,), F32)
    bias = jnp.concatenate([fox_f_bias[l], gdn_dt_bias[l], zeros8, jnp.zeros((pad,), F32)])[None]
    alog = jnp.concatenate([zeros8, gdn_A_log[l], zeros8, jnp.zeros((pad,), F32)])[None]
    col, row, z_g = _gates(h3, w_s, w[:, _OFF_ZG:], bias, alog, min(1024, s))
    row = row.reshape(bsz, row.shape[1], 1, s)

    o_f = _attn(qkv_f, col, row, fox_out_norm[l][None], min(4096, s), 256, 1024)
    o_g = _gdn(qkv_g, z_g, col, row, gdn_out_norm[l][None], 256)

    x2 = _oproj(o_f.reshape(n, GROUP_W), o_g.reshape(n, GROUP_W), bf(w_out[l]), x1, mod, tiles_per_batch, tm)

    out = _ffn(x2, vecs2, tiles_per_batch, tm, *ffn_w, 1, final=True)
    return out.reshape(bsz, s, d)
```

```python
import functools
import math

import jax
import jax.numpy as jnp
from jax import lax
from jax.experimental import pallas as pl
from jax.experimental.pallas import tpu as pltpu

F32 = jnp.float32
BF16 = jnp.bfloat16

D_MODEL = 2048
HEAD_DIM = 128
N_HEADS = 8
GROUP_W = N_HEADS * HEAD_DIM
D_FF = 5632
CONV_W = 4
N_MOD = 9
EPS = 1e-6
LANES = 128
SUBLANES = 8
GDN_CHUNK = 128
LOG2E = math.log2(math.e)

_OFF_QF = 0
_OFF_FF = 3 * GROUP_W
_OFF_QKVG = _OFF_FF + N_HEADS
_OFF_AG = _OFF_QKVG + 3 * GROUP_W
_OFF_BG = _OFF_AG + N_HEADS
_OFF_ZG = _OFF_BG + N_HEADS

VMEM_LIMIT = 60 * 1024 * 1024


def _params(*sem):
    return pltpu.CompilerParams(dimension_semantics=sem, vmem_limit_bytes=VMEM_LIMIT)


def _sigmoid(x):
    return 1.0 / (1.0 + jnp.exp(-x))


def _silu(x):
    return x * _sigmoid(x)


def _softplus(x):
    return jnp.maximum(x, 0.0) + jnp.log1p(jnp.exp(-jnp.abs(x)))


def _dot(a, b):
    return jnp.dot(a, b, preferred_element_type=F32)


def _dot_nt(a, b):
    return lax.dot_general(a, b, (((1,), (1,)), ((), ())), preferred_element_type=F32)


def _dot_tn(a, b):
    return lax.dot_general(a, b, (((0,), (0,)), ((), ())), preferred_element_type=F32)


def _rms(x):
    return x * lax.rsqrt(jnp.mean(x * x, axis=-1, keepdims=True) + EPS)


def _ada_kernel(c_ref, w_ref, b_ref, o_ref):
    cond = _silu(c_ref[...])
    o_ref[...] = _dot(cond.astype(BF16), w_ref[...].astype(BF16)) + b_ref[...]


def _ada(c_pad, w, b, layer):
    rows, d = c_pad.shape
    n = w.shape[2]
    tn = 1024
    return pl.pallas_call(
        _ada_kernel,
        grid=(n // tn,),
        in_specs=[pl.BlockSpec((rows, d), lambda j: (0, 0)),
                  pl.BlockSpec((None, d, tn), lambda j: (layer, 0, j)),
                  pl.BlockSpec((1, tn), lambda j: (layer, j))],
        out_specs=pl.BlockSpec((rows, tn), lambda j: (0, j)),
        out_shape=jax.ShapeDtypeStruct((rows, n), F32),
        compiler_params=_params("arbitrary"),
        name="ada",
    )(c_pad, w, b)


_ROW_CHUNK = 64


def _inv_rms(x):
    return lax.rsqrt(jnp.mean(x * x, axis=-1, keepdims=True) + EPS)


def _ffn_kernel(final, x_ref, vec_ref, wg_ref, wu_ref, wd_ref, *rest):
    if final:
        o_ref, h_scr, acc_scr, r_scr = rest
    else:
        o_ref, h2_ref, h_scr, acc_scr, r_scr = rest
    f = pl.program_id(1)
    last = pl.num_programs(1) - 1
    tm = x_ref.shape[0]
    vec = lambda r: vec_ref[0, r:r + 1, :]
    halves = [slice(0, tm // 2), slice(tm // 2, tm)]

    def chunks(rows):
        return [slice(r, r + _ROW_CHUNK) for r in range(rows.start, rows.stop, _ROW_CHUNK)]

    def swiglu_down(rows):
        h = h_scr[rows, :]
        act = _silu(_dot(h, wg_ref[...])) * _dot(h, wu_ref[...])
        return _dot(act.astype(BF16), wd_ref[...])

    @pl.when(f == 0)
    def _():
        gain = vec(0) * (1.0 + vec(2))
        shift = vec(1)
        for rows in halves:
            for c in chunks(rows):
                r_scr[c, :] = _inv_rms(x_ref[c, :])
            for c in chunks(rows):
                h_scr[c, :] = (x_ref[c, :] * r_scr[c, :] * gain + shift).astype(BF16)
            acc_scr[rows, :] = swiglu_down(rows)

    @pl.when((f > 0) & (f < last))
    def _():
        acc_scr[...] += swiglu_down(slice(0, tm))

    @pl.when(f == last)
    def _():
        half_gate = 0.5 * vec(3)
        gain2 = vec(4) if final else vec(4) * (1.0 + vec(6))
        for rows in halves:
            acc_scr[rows, :] += swiglu_down(rows)
            for c in chunks(rows):
                xn = x_ref[c, :] + half_gate * acc_scr[c, :]
                o_ref[c, :] = xn
                r_scr[c, :] = _inv_rms(xn)
            for c in chunks(rows):
                y = o_ref[c, :] * r_scr[c, :] * gain2
                if final:
                    o_ref[c, :] = y
                else:
                    h2_ref[c, :] = (y + vec(5)).astype(BF16)


_FFN_TILES = {False: (512, 512), True: (1024, 256)}


def _ffn(x, vecs, seq, wg, wu, wd, which, final):
    n, d = x.shape
    dff = wd.shape[1]
    tm, tf = _FFN_TILES[final]
    tm = min(tm, seq)
    tiles_per_batch = seq // tm
    row = lambda i, f: (i, 0)
    in_specs = [pl.BlockSpec((tm, d), row),
                pl.BlockSpec((1, SUBLANES, d), lambda i, f: (i // tiles_per_batch, 0, 0)),
                pl.BlockSpec((None, d, tf), lambda i, f: (which, 0, f)),
                pl.BlockSpec((None, d, tf), lambda i, f: (which, 0, f)),
                pl.BlockSpec((None, tf, d), lambda i, f: (which, f, 0))]
    if final:
        out_shape = jax.ShapeDtypeStruct((n, d), F32)
        out_specs = pl.BlockSpec((tm, d), row)
    else:
        out_shape = (jax.ShapeDtypeStruct((n, d), F32), jax.ShapeDtypeStruct((n, d), BF16))
        out_specs = (pl.BlockSpec((tm, d), row), pl.BlockSpec((tm, d), row))
    return pl.pallas_call(
        functools.partial(_ffn_kernel, final),
        grid=(n // tm, dff // tf),
        in_specs=in_specs,
        out_specs=out_specs,
        out_shape=out_shape,
        scratch_shapes=[pltpu.VMEM((tm, d), BF16), pltpu.VMEM((tm, d), F32), pltpu.VMEM((tm, 1), F32)],
        compiler_params=_params("arbitrary", "arbitrary"),
        name="ffn_final" if final else "ffn",
    )(x, vecs, wg, wu, wd)


_CONV_COLS = 2 * HEAD_DIM


def _proj_g_kernel(h_ref, w_ref, wf_ref, cw_ref, o_ref, of_ref, prev_scr, *bufs):
    j = pl.program_id(0)
    s = pl.program_id(2)
    tm = h_ref.shape[0]
    h = h_ref[...]
    cw = cw_ref[...]
    prev = jnp.where(s == 0, 0.0, prev_scr[...])
    gain = jnp.where(j == 0, HEAD_DIM ** -0.5, 1.0)
    for p, buf in enumerate(bufs):
        cols = slice(p * _CONV_COLS, (p + 1) * _CONV_COLS)
        y = _dot(h, w_ref[:, cols])
        buf[0:SUBLANES, :] = prev[:, cols]
        buf[SUBLANES:, :] = y
        prev_scr[:, cols] = y[tm - SUBLANES:, :]
        out = y * cw[CONV_W - 1:CONV_W, cols]
        for sh in range(1, CONV_W):
            out = out + buf[SUBLANES - sh:SUBLANES - sh + tm, :] * cw[CONV_W - 1 - sh:CONV_W - sh, cols]
        out = _silu(out)
        for hh in range(_CONV_COLS // HEAD_DIM):
            t = out[:, hh * HEAD_DIM:(hh + 1) * HEAD_DIM]
            nrm = lax.rsqrt(jnp.sum(t * t, axis=-1, keepdims=True) + EPS) * gain
            c0 = p * _CONV_COLS + hh * HEAD_DIM
            o_ref[0, :, c0:c0 + HEAD_DIM] = t * jnp.where(j < 2, nrm, 1.0)
        yf = _dot(h, wf_ref[:, cols]) * jnp.where(j == 0, LOG2E / math.sqrt(HEAD_DIM), 1.0)
        of_ref[0, :, cols] = yf.astype(of_ref.dtype)


def _proj_g(h3, w, wf, cw, tm):
    b, s, d = h3.shape
    c = w.shape[1]
    tn = GROUP_W
    n_pieces = tn // _CONV_COLS
    return pl.pallas_call(
        _proj_g_kernel,
        grid=(c // tn, b, s // tm),
        in_specs=[pl.BlockSpec((None, tm, d), lambda j, bb, i: (bb, i, 0)),
                  pl.BlockSpec((d, tn), lambda j, bb, i: (0, j)),
                  pl.BlockSpec((d, tn), lambda j, bb, i: (0, j)),
                  pl.BlockSpec((SUBLANES, tn), lambda j, bb, i: (0, j))],
        out_specs=(pl.BlockSpec((1, tm, tn), lambda j, bb, i: (bb, i, j)),
                   pl.BlockSpec((1, tm, tn), lambda j, bb, i: (bb, i, j))),
        out_shape=(jax.ShapeDtypeStruct((b, s, c), F32), jax.ShapeDtypeStruct((b, s, c), BF16)),
        scratch_shapes=[pltpu.VMEM((SUBLANES, tn), F32)]
        + [pltpu.VMEM((tm + SUBLANES, _CONV_COLS), F32) for _ in range(n_pieces)],
        compiler_params=_params("arbitrary", "arbitrary", "arbitrary"),
        name="proj_g",
    )(h3, w, wf, cw)


def _gates_kernel(h_ref, w_ref, wz_ref, bias_ref, alog_ref, col_ref, row_ref, z_ref, carry_scr):
    s = pl.program_id(1)
    h = h_ref[...]
    z_ref[0] = _silu(_dot(h, wz_ref[...]))
    y = _dot(h, w_ref[...])
    tm = y.shape[0]
    lane = lax.broadcasted_iota(jnp.int32, (1, LANES), 1)
    is_f = lane < N_HEADS
    is_a = (lane >= N_HEADS) & (lane < 2 * N_HEADS)
    yb = y + bias_ref[...]
    sp = _softplus(jnp.where(is_f, -yb, yb))
    val = jnp.where(is_f, -sp, jnp.where(is_a, -jnp.exp(alog_ref[...]) * sp, _sigmoid(y)))

    @pl.when(s == 0)
    def _():
        carry_scr[...] = jnp.zeros_like(carry_scr)

    rowi = lax.broadcasted_iota(jnp.int32, (tm, LANES), 0)
    rowc = rowi % GDN_CHUNK
    cum = val
    cumc = val
    sh = 1
    while sh < tm:
        cum = cum + jnp.where(rowi >= sh, pltpu.roll(cum, sh, 0), 0.0)
        if sh < GDN_CHUNK:
            cumc = cumc + jnp.where(rowc >= sh, pltpu.roll(cumc, sh, 0), 0.0)
        sh *= 2
    cum = cum + carry_scr[0:1, :]
    carry_scr[...] = jnp.broadcast_to(cum[tm - 1:tm, :], carry_scr.shape)
    out = jnp.where(is_f, cum * LOG2E, jnp.where(is_a, cumc, val))
    col_ref[0] = out
    row_ref[0] = out.T[:row_ref.shape[1], :]


def _gates(h3, w, wz, bias, alog, tm):
    b, s, d = h3.shape
    nrow = 32
    cz = wz.shape[1]
    return pl.pallas_call(
        _gates_kernel,
        grid=(b, s // tm),
        in_specs=[pl.BlockSpec((None, tm, d), lambda bb, i: (bb, i, 0)),
                  pl.BlockSpec((d, LANES), lambda bb, i: (0, 0)),
                  pl.BlockSpec((d, cz), lambda bb, i: (0, 0)),
                  pl.BlockSpec((1, LANES), lambda bb, i: (0, 0)),
                  pl.BlockSpec((1, LANES), lambda bb, i: (0, 0))],
        out_specs=(pl.BlockSpec((1, tm, LANES), lambda bb, i: (bb, i, 0)),
                   pl.BlockSpec((1, nrow, tm), lambda bb, i: (bb, 0, i)),
                   pl.BlockSpec((1, tm, cz), lambda bb, i: (bb, i, 0))),
        out_shape=(jax.ShapeDtypeStruct((b, s, LANES), F32),
                   jax.ShapeDtypeStruct((b, nrow, s), F32),
                   jax.ShapeDtypeStruct((b, s, cz), F32)),
        scratch_shapes=[pltpu.VMEM((SUBLANES, LANES), F32)],
        compiler_params=_params("arbitrary", "arbitrary"),
        name="gates",
    )(h3, w, wz, bias, alog)


def _pick_lane(x, idx):
    lane = lax.broadcasted_iota(jnp.int32, x.shape, 1)
    return jnp.sum(jnp.where(lane == idx, x, 0.0), axis=-1, keepdims=True)


def _attn_kernel(tq, rg, tk, q_ref, k_ref, v_ref, cq_ref, ck_ref, gn_ref, o_ref, m_scr, acc_scr):
    h = pl.program_id(1)
    qi = pl.program_id(2)
    ng = tq // rg
    ahead = 3
    cq_all = _pick_lane(cq_ref[0], h)
    qs = [q_ref[0, g * rg:(g + 1) * rg, :] for g in range(ng)]
    cqs = [cq_all[g * rg:(g + 1) * rg] for g in range(ng)]
    q0 = pl.multiple_of(qi * tq, tq)

    def step(off, widths, first):
        wmax = max(widths)
        k = k_ref[0, pl.ds(off, wmax), :]
        v1 = jnp.concatenate([v_ref[0, pl.ds(off, wmax), :], jnp.ones((wmax, HEAD_DIM), BF16)], axis=1)
        ck = ck_ref[0, 0, :, pl.ds(off, wmax)]
        ss = {g: _dot_nt(qs[g], k[:widths[g]]) for g in range(min(ahead, ng))}
        for g in range(ng):
            w = widths[g]
            rows = slice(g * rg, (g + 1) * rg)
            if g + ahead < ng:
                ss[g + ahead] = _dot_nt(qs[g + ahead], k[:widths[g + ahead]])
            s = ss.pop(g) - ck[:, :w]
            if first:
                r = lax.broadcasted_iota(jnp.int32, (rg, w), 0) + g * rg
                c = lax.broadcasted_iota(jnp.int32, (rg, w), 1)
                s = jnp.where(c <= r, s, -jnp.inf)
                m_new = jnp.max(s, axis=-1, keepdims=True) + cqs[g]
            else:
                m_old = m_scr[rows, :]
                m_new = jnp.maximum(m_old, jnp.max(s, axis=-1, keepdims=True) + cqs[g])
            p = jnp.exp2(s - (m_new - cqs[g]))
            pv = _dot(p.astype(BF16), v1[:w])
            acc_scr[rows, :] = pv if first else jnp.exp2(m_old - m_new) * acc_scr[rows, :] + pv
            m_scr[rows, :] = m_new

    step(q0, [(g + 1) * rg for g in range(ng)], True)

    def body(j, carry):
        step(pl.multiple_of(j * tk, tk), [tk] * ng, False)
        return carry

    lax.fori_loop(0, qi * (tq // tk), body, 0)
    for g in range(ng):
        rows = slice(g * rg, (g + 1) * rg)
        o = acc_scr[rows, :HEAD_DIM] / acc_scr[rows, HEAD_DIM:]
        o_ref[0, rows, :] = (_rms(o) * gn_ref[...]).astype(o_ref.dtype)


def _attn(qkv, col, row, gn, tq, rg, tk):
    b, s, _ = qkv.shape
    return pl.pallas_call(
        functools.partial(_attn_kernel, tq, rg, tk),
        grid=(b, N_HEADS, s // tq),
        in_specs=[pl.BlockSpec((1, tq, HEAD_DIM), lambda bb, h, i: (bb, i, h)),
                  pl.BlockSpec((1, s, HEAD_DIM), lambda bb, h, i: (bb, 0, N_HEADS + h)),
                  pl.BlockSpec((1, s, HEAD_DIM), lambda bb, h, i: (bb, 0, 2 * N_HEADS + h)),
                  pl.BlockSpec((1, tq, LANES), lambda bb, h, i: (bb, i, 0)),
                  pl.BlockSpec((1, 1, 1, s), lambda bb, h, i: (bb, h, 0, 0)),
                  pl.BlockSpec((1, HEAD_DIM), lambda bb, h, i: (0, 0))],
        out_specs=pl.BlockSpec((1, tq, HEAD_DIM), lambda bb, h, i: (bb, i, h)),
        out_shape=jax.ShapeDtypeStruct((b, s, GROUP_W), BF16),
        scratch_shapes=[pltpu.VMEM((tq, 1), F32), pltpu.VMEM((tq, 2 * HEAD_DIM), F32)],
        compiler_params=_params("arbitrary", "arbitrary", "arbitrary"),
        name="attn",
    )(qkv, qkv, qkv, col, row, gn)


def _unit_lower_inverses(lowers):
    c = lowers[0].shape[0]
    r = lax.broadcasted_iota(jnp.int32, (c, c), 0)
    cc = lax.broadcasted_iota(jnp.int32, (c, c), 1)
    eye = jnp.where(r == cc, 1.0, 0.0)
    pair = (r ^ cc) == 1
    invs = [eye - jnp.where(pair, low, 0.0) for low in lowers]
    level = 1
    while (1 << level) < c:
        sel = ((r >> level) ^ (cc >> level)) == 1
        inv_b = [inv.astype(BF16) for inv in invs]
        half = [_dot(ib, jnp.where(sel, low, 0.0).astype(BF16)).astype(BF16) for ib, low in zip(inv_b, lowers)]
        invs = [inv - _dot(hf, ib) for inv, hf, ib in zip(invs, half, inv_b)]
        level += 1
    return invs


def _gdn_kernel(t_tokens, q_ref, k_ref, v_ref, z_ref, col_ref, row_ref, gn_ref, o_ref, state_scr):
    t = pl.program_id(1)
    c = GDN_CHUNK
    n_chunks = t_tokens // c

    @pl.when(t == 0)
    def _():
        state_scr[...] = jnp.zeros_like(state_scr)

    col = col_ref[0]
    r = lax.broadcasted_iota(jnp.int32, (c, c), 0)
    cc = lax.broadcasted_iota(jnp.int32, (c, c), 1)
    incl = cc <= r
    strict = cc < r
    chains = [(hh, ci) for ci in range(n_chunks) for hh in range(N_HEADS)]
    g_all = [_pick_lane(col, N_HEADS + hh) for hh in range(N_HEADS)]
    b_all = [_pick_lane(col, 2 * N_HEADS + hh) for hh in range(N_HEADS)]

    def tile(ref, hh, ci):
        return ref[0, ci * c:(ci + 1) * c, hh * HEAD_DIM:(hh + 1) * HEAD_DIM]

    lowers, a_intra = [], []
    for hh, ci in chains:
        k = tile(k_ref, hh, ci)
        g = g_all[hh][ci * c:(ci + 1) * c]
        g_row = row_ref[0, hh, :, ci * c:(ci + 1) * c]
        decay = jnp.exp(jnp.where(incl, g - g_row, -jnp.inf))
        k_b = k.astype(BF16)
        kb_b = (k * b_all[hh][ci * c:(ci + 1) * c]).astype(BF16)
        lowers.append(jnp.where(strict, _dot_nt(kb_b, k_b) * decay, 0.0))
        a_intra.append(jnp.where(incl, _dot_nt(tile(q_ref, hh, ci).astype(BF16), k_b) * decay, 0.0).astype(BF16))
    t_invs = [inv.astype(BF16) for inv in _unit_lower_inverses(lowers)]
    us, ws = [], []
    for (hh, ci), t_inv in zip(chains, t_invs):
        g = g_all[hh][ci * c:(ci + 1) * c]
        beta = b_all[hh][ci * c:(ci + 1) * c]
        us.append(_dot(t_inv, (tile(v_ref, hh, ci) * beta).astype(BF16)))
        ws.append(_dot(t_inv, (tile(k_ref, hh, ci) * (beta * jnp.exp(g))).astype(BF16)).astype(BF16))

    states = [state_scr[hh] for hh in range(N_HEADS)]
    for ci in range(n_chunks):
        idxs = [chains.index((hh, ci)) for hh in range(N_HEADS)]
        gs = [g_all[hh][ci * c:(ci + 1) * c] for hh in range(N_HEADS)]
        state_bs = [st.astype(BF16) for st in states]
        w_s = [_dot(ws[i], sb) for i, sb in zip(idxs, state_bs)]
        q_s = [_dot((tile(q_ref, hh, ci) * jnp.exp(gs[hh])).astype(BF16), state_bs[hh]) for hh in range(N_HEADS)]
        v_new_bs = [(us[i] - w_s[hh]).astype(BF16) for hh, i in enumerate(idxs)]
        for hh, i in enumerate(idxs):
            g = gs[hh]
            g_last = g[c - 1:c, :]
            o = q_s[hh] + _dot(a_intra[i], v_new_bs[hh])
            k_dec = (tile(k_ref, hh, ci) * jnp.exp(g_last - g)).astype(BF16)
            states[hh] = states[hh] * jnp.exp(g_last) + _dot_tn(k_dec, v_new_bs[hh])
            o_ref[0, ci * c:(ci + 1) * c, hh * HEAD_DIM:(hh + 1) * HEAD_DIM] = (
                _rms(o) * gn_ref[...] * tile(z_ref, hh, ci)).astype(o_ref.dtype)
    for hh in range(N_HEADS):
        state_scr[hh] = states[hh]


def _gdn(qkv, z, col, row, gn, t_tokens):
    b, s, _ = qkv.shape
    group = lambda g: pl.BlockSpec((1, t_tokens, GROUP_W), lambda bb, i: (bb, i, g))
    return pl.pallas_call(
        functools.partial(_gdn_kernel, t_tokens),
        grid=(b, s // t_tokens),
        in_specs=[group(0), group(1), group(2), group(0),
                  pl.BlockSpec((1, t_tokens, LANES), lambda bb, i: (bb, i, 0)),
                  pl.BlockSpec((1, N_HEADS, 1, t_tokens), lambda bb, i: (bb, 1, 0, i)),
                  pl.BlockSpec((1, HEAD_DIM), lambda bb, i: (0, 0))],
        out_specs=pl.BlockSpec((1, t_tokens, GROUP_W), lambda bb, i: (bb, i, 0)),
        out_shape=jax.ShapeDtypeStruct((b, s, GROUP_W), BF16),
        scratch_shapes=[pltpu.VMEM((N_HEADS, HEAD_DIM, HEAD_DIM), F32)],
        compiler_params=_params("arbitrary", "arbitrary"),
        name="gdn",
    )(qkv, qkv, qkv, z, col, row, gn)


def _oproj_kernel(of_ref, og_ref, wf_ref, wg_ref, x_ref, gt_ref, o_ref):
    y = _dot(of_ref[...], wf_ref[...]) + _dot(og_ref[...], wg_ref[...])
    o_ref[...] = x_ref[...] + gt_ref[0] * y


def _oproj(o_f, o_g, w, x, mod, tiles_per_batch, tm):
    n, d = x.shape
    row = lambda i: (i, 0)
    return pl.pallas_call(
        _oproj_kernel,
        grid=(n // tm,),
        in_specs=[pl.BlockSpec((tm, GROUP_W), row), pl.BlockSpec((tm, GROUP_W), row),
                  pl.BlockSpec((GROUP_W, d), lambda i: (0, 0)), pl.BlockSpec((GROUP_W, d), lambda i: (1, 0)),
                  pl.BlockSpec((tm, d), row),
                  pl.BlockSpec((1, 1, d), lambda i: ((i // tiles_per_batch) * N_MOD + 5, 0, 0))],
        out_specs=pl.BlockSpec((tm, d), row),
        out_shape=jax.ShapeDtypeStruct((n, d), F32),
        compiler_params=_params("arbitrary"),
        name="oproj",
    )(o_f, o_g, w, w, x, mod)


def kernel(x, c, ada_w, ada_b, norm_g, ffn_w_gate, ffn_w_up, ffn_w_down, w_in, w_out,
           fox_f_bias, fox_out_norm, gdn_conv, gdn_A_log, gdn_dt_bias, gdn_out_norm, final_norm):
    bsz, s, d = x.shape
    n = bsz * s
    tm = 512
    tiles_per_batch = s // tm
    l = 0
    bf = lambda t: t.astype(BF16)

    c_pad = jnp.zeros((SUBLANES, d), F32).at[:bsz].set(c)
    mod9 = _ada(c_pad, ada_w, ada_b, l)[:bsz].reshape(bsz, N_MOD, d)
    mod = mod9.reshape(bsz * N_MOD, 1, d)
    rep = lambda v: jnp.broadcast_to(v[None, None, :], (bsz, 1, d))
    zero = jnp.zeros((bsz, 1, d), F32)
    vecs1 = jnp.concatenate([rep(norm_g[l, 0]), mod9[:, 0:3], rep(norm_g[l, 1]), mod9[:, 3:5], zero], axis=1)
    vecs2 = jnp.concatenate([rep(norm_g[l, 2]), mod9[:, 6:9], rep(final_norm), zero, zero, zero], axis=1)

    ffn_w = (bf(ffn_w_gate[l]), bf(ffn_w_up[l]), bf(ffn_w_down[l]))

    xf = x.reshape(n, d)
    x1, h2 = _ffn(xf, vecs1, s, *ffn_w, 0, final=False)

    w = bf(w_in[l])
    h3 = h2.reshape(bsz, s, d)
    cw = jnp.zeros((SUBLANES, 3 * GROUP_W), F32).at[:CONV_W].set(gdn_conv[l])
    qkv_g, qkv_f = _proj_g(h3, w[:, _OFF_QKVG:_OFF_AG], w, cw, min(1024, s))

    pad = LANES - 3 * N_HEADS
    w_s = jnp.concatenate([w[:, _OFF_FF:_OFF_QKVG], w[:, _OFF_AG:_OFF_ZG], jnp.zeros((d, pad), BF16)], axis=1)
    zeros8 = jnp.zeros((N_HEADS,), F32)
    bias = jnp.concatenate([fox_f_bias[l], gdn_dt_bias[l], zeros8, jnp.zeros((pad,), F32)])[None]
    alog = jnp.concatenate([zeros8, gdn_A_log[l], zeros8, jnp.zeros((pad,), F32)])[None]
    col, row, z_g = _gates(h3, w_s, w[:, _OFF_ZG:], bias, alog, min(1024, s))
    row = row.reshape(bsz, row.shape[1], 1, s)

    o_f = _attn(qkv_f, col, row, fox_out_norm[l][None], min(4096, s), 256, 1024)
    o_g = _gdn(qkv_g, z_g, col, row, gdn_out_norm[l][None], 256)

    x2 = _oproj(o_f.reshape(n, GROUP_W), o_g.reshape(n, GROUP_W), bf(w_out[l]), x1, mod, tiles_per_batch, tm)

    out = _ffn(x2, vecs2, s, *ffn_w, 1, final=True)
    return out.reshape(bsz, s, d)
```
